```python
import math
import jax, jax.numpy as jnp
from jax import lax
import numpy as np

D_MODEL = 2048
BATCH = 2
SEQ = 4096
DEPTH = 4

D_MIX = D_MODEL
D_SSD = D_MIX // 2
D_RET = D_MIX - D_SSD
SSD_HEAD_DIM = 64
SSD_HEADS = D_SSD // SSD_HEAD_DIM
SSD_GROUPS = 2
SSD_STATE = 128
SSD_CONV = 4
SSD_XBC = D_SSD + 2 * SSD_GROUPS * SSD_STATE
RET_HEADS = 8
RET_HEAD_DIM = D_RET // RET_HEADS
RET_QK = RET_HEADS * RET_HEAD_DIM
CHUNK = 128
ROPE_BASE = 10000.0
D_FF = 5504
N_MOD = 9
EPS = 1e-6
IN_SPLITS = (D_SSD, SSD_XBC, SSD_HEADS, RET_QK, RET_QK, D_RET, D_RET)
D_IN_PROJ = sum(IN_SPLITS)

kernel_name = "hybrid_ssd_retention_macaron_adaln"


def rms_norm(x, w):
    xf = x.astype(jnp.float32)
    y = xf * lax.rsqrt(jnp.mean(xf * xf, axis=-1, keepdims=True) + EPS)
    return (y * w.astype(jnp.float32)).astype(x.dtype)


def modulate(x, w, shift, scale):
    return rms_norm(x, w) * (1.0 + scale[:, None, :]) + shift[:, None, :]


def swiglu(h, w13, w2):
    a, b = jnp.split(h @ w13, 2, axis=-1)
    return (jax.nn.silu(a) * b) @ w2


def scan_chunk_states(states, decay):
    decay = decay.astype(states.dtype)

    def step(carry, inp):
        s_c, d_c = inp
        return carry * d_c + s_c, carry

    init = jnp.zeros_like(states[:, 0])
    _, prev = lax.scan(step, init, (jnp.moveaxis(states, 1, 0), jnp.moveaxis(decay, 1, 0)))
    return jnp.moveaxis(prev, 0, 1)


def causal_depthwise_conv(x, w, b):
    ch = x.shape[-1]
    y = lax.conv_general_dilated(x, w.reshape(SSD_CONV, 1, ch), window_strides=(1,),
                                 padding=[(SSD_CONV - 1, 0)],
                                 dimension_numbers=("NWC", "WIO", "NWC"),
                                 feature_group_count=ch)
    return y + b


def ssd_chunked(x, dt, A, Bm, Cm):
    bsz, L, h, p = x.shape
    g, n = Bm.shape[-2:]
    r = h // g
    nc = L // CHUNK
    f32 = jnp.float32
    xdt = (x.astype(f32) * dt[..., None]).reshape(bsz, nc, CHUNK, g, r, p)
    a = jnp.transpose((dt * A).reshape(bsz, nc, CHUNK, g, r), (0, 1, 3, 4, 2))
    Bc = Bm.astype(f32).reshape(bsz, nc, CHUNK, g, n)
    Cc = Cm.astype(f32).reshape(bsz, nc, CHUNK, g, n)
    a_cum = jnp.cumsum(a, axis=-1)
    idx = jnp.arange(CHUNK)
    causal = idx[:, None] >= idx[None, :]
    seg = a_cum[..., :, None] - a_cum[..., None, :]
    lmat = jnp.exp(jnp.where(causal, seg, -jnp.inf))
    cb = jnp.einsum("bclgn,bcsgn->bcgls", Cc, Bc)
    y_diag = jnp.einsum("bcgrls,bcsgrp->bclgrp", cb[:, :, :, None] * lmat, xdt)
    decay_states = jnp.exp(a_cum[..., -1:] - a_cum)
    states = jnp.einsum("bclgn,bcgrl,bclgrp->bcgrpn", Bc, decay_states, xdt)
    prev = scan_chunk_states(states, jnp.exp(a_cum[..., -1])[..., None, None])
    y_off = jnp.einsum("bclgn,bcgrpn,bcgrl->bclgrp", Cc, prev, jnp.exp(a_cum))
    return (y_diag + y_off).reshape(bsz, L, h, p).astype(x.dtype)


def ssd_mixer(z, xbc, dt_raw, conv_w, conv_b, dt_bias, a_log, d_skip, norm_w):
    bsz, L, _ = z.shape
    xbc = jax.nn.silu(causal_depthwise_conv(xbc, conv_w, conv_b))
    xs, Bm, Cm = jnp.split(xbc, [D_SSD, D_SSD + SSD_GROUPS * SSD_STATE], axis=-1)
    dt = jax.nn.softplus((dt_raw + dt_bias).astype(jnp.float32))
    A = -jnp.exp(a_log.astype(jnp.float32))
    xh = xs.reshape(bsz, L, SSD_HEADS, SSD_HEAD_DIM)
    y = ssd_chunked(xh, dt, A,
                    Bm.reshape(bsz, L, SSD_GROUPS, SSD_STATE),
                    Cm.reshape(bsz, L, SSD_GROUPS, SSD_STATE))
    y = (y + xh * d_skip[:, None]).reshape(bsz, L, D_SSD)
    return rms_norm(y * jax.nn.silu(z), norm_w)


def rotary(x):
    L, d = x.shape[1], x.shape[-1]
    inv = ROPE_BASE ** (-jnp.arange(0, d, 2, dtype=jnp.float32) / d)
    ang = jnp.arange(L, dtype=jnp.float32)[:, None] * inv[None, :]
    cos, sin = jnp.cos(ang)[None, :, None, :], jnp.sin(ang)[None, :, None, :]
    xf = x.astype(jnp.float32)
    x1, x2 = jnp.split(xf, 2, axis=-1)
    return jnp.concatenate([x1 * cos - x2 * sin, x1 * sin + x2 * cos], axis=-1).astype(x.dtype)


def retention_mixer(q, k, v, gt, gn_w, gn_b):
    bsz, L, _ = q.shape
    nc = L // CHUNK
    dt_ = q.dtype
    q = rotary(q.reshape(bsz, L, RET_HEADS, RET_HEAD_DIM))
    k = rotary(k.reshape(bsz, L, RET_HEADS, RET_HEAD_DIM)) * (RET_HEAD_DIM ** -0.5)
    v = v.reshape(bsz, L, RET_HEADS, RET_HEAD_DIM)
    qc = q.reshape(bsz, nc, CHUNK, RET_HEADS, RET_HEAD_DIM)
    kc = k.reshape(bsz, nc, CHUNK, RET_HEADS, RET_HEAD_DIM)
    vc = v.reshape(bsz, nc, CHUNK, RET_HEADS, RET_HEAD_DIM)
    log_g = jnp.log(1.0 - 2.0 ** (-5.0 - jnp.arange(RET_HEADS, dtype=jnp.float32)))
    idx = jnp.arange(CHUNK, dtype=jnp.float32)
    diff = idx[:, None] - idx[None, :]
    inner_decay = jnp.where(diff >= 0, jnp.exp(jnp.maximum(diff, 0.0)[None] * log_g[:, None, None]), 0.0)
    scores = jnp.einsum("bclhd,bcshd->bchls", qc, kc) * inner_decay.astype(dt_)
    inner = jnp.einsum("bchls,bcshe->bclhe", scores, vc)
    k_decay = jnp.exp((CHUNK - 1.0 - idx)[None, :] * log_g[:, None]).astype(dt_)
    kv = jnp.einsum("bcshd,hs,bcshe->bchde", kc, k_decay, vc)
    chunk_decay = jnp.broadcast_to(jnp.exp(CHUNK * log_g)[None, None, :, None, None],
                                   (bsz, nc, RET_HEADS, 1, 1))
    prev = scan_chunk_states(kv, chunk_decay)
    q_decay = jnp.exp((idx + 1.0)[None, :] * log_g[:, None]).astype(dt_)
    cross = jnp.einsum("bclhd,hl,bchde->bclhe", qc, q_decay, prev)
    o = (inner + cross).reshape(bsz, L, RET_HEADS, RET_HEAD_DIM)
    of = o.astype(jnp.float32)
    mu = jnp.mean(of, axis=-1, keepdims=True)
    var = jnp.mean(jnp.square(of - mu), axis=-1, keepdims=True)
    on = ((of - mu) * lax.rsqrt(var + EPS)).reshape(bsz, L, D_RET)
    on = (on * gn_w.astype(jnp.float32) + gn_b.astype(jnp.float32)).astype(dt_)
    return jax.nn.silu(gt) * on


def setup_inputs(seed: int = 0) -> dict:
    key = jax.random.key(seed)
    ks = jax.random.split(key, 20)
    f32 = jnp.float32
    nrm = lambda k, shape, s: jax.random.normal(k, shape, f32) * s
    dt0 = jnp.exp(jax.random.uniform(ks[9], (DEPTH, SSD_HEADS), f32, math.log(1e-3), math.log(1e-1)))
    return {
        "x": nrm(ks[0], (BATCH, SEQ, D_MODEL), 1.0),
        "c": nrm(ks[1], (BATCH, D_MODEL), 1.0),
        "norm_w": 1.0 + nrm(ks[2], (DEPTH, 3, D_MODEL), 0.02),
        "mod_w": nrm(ks[3], (DEPTH, D_MODEL, N_MOD * D_MODEL), D_MODEL ** -0.5),
        "mod_b": nrm(ks[4], (DEPTH, N_MOD * D_MODEL), 0.02),
        "ffn_w13": nrm(ks[5], (DEPTH, 2, D_MODEL, 2 * D_FF), D_MODEL ** -0.5),
        "ffn_w2": nrm(ks[6], (DEPTH, 2, D_FF, D_MODEL), D_FF ** -0.5),
        "w_in": nrm(ks[7], (DEPTH, D_MODEL, D_IN_PROJ), D_MODEL ** -0.5),
        "conv_w": nrm(ks[8], (DEPTH, SSD_CONV, SSD_XBC), SSD_CONV ** -0.5),
        "conv_b": nrm(ks[10], (DEPTH, SSD_XBC), 0.02),
        "dt_bias": dt0 + jnp.log(-jnp.expm1(-dt0)),
        "a_log": jnp.log(jax.random.uniform(ks[11], (DEPTH, SSD_HEADS), f32, 1.0, 16.0)),
        "d_skip": 1.0 + nrm(ks[12], (DEPTH, SSD_HEADS), 0.02),
        "ssd_norm_w": 1.0 + nrm(ks[13], (DEPTH, D_SSD), 0.02),
        "ret_norm_w": 1.0 + nrm(ks[14], (DEPTH, D_RET), 0.02),
        "ret_norm_b": nrm(ks[15], (DEPTH, D_RET), 0.02),
        "w_out": nrm(ks[16], (DEPTH, D_MIX, D_MODEL), D_MIX ** -0.5),
        "final_norm_w": 1.0 + nrm(ks[17], (D_MODEL,), 0.02),
    }


def reference(x, c, norm_w, mod_w, mod_b, ffn_w13, ffn_w2, w_in, conv_w, conv_b, dt_bias, a_log,
              d_skip, ssd_norm_w, ret_norm_w, ret_norm_b, w_out, final_norm_w):
    split_at = list(np.cumsum(IN_SPLITS)[:-1])
    c_act = jax.nn.silu(c)
    for l in range(DEPTH):
        mods = (c_act @ mod_w[l] + mod_b[l]).reshape(c.shape[0], N_MOD, D_MODEL)
        h = modulate(x, norm_w[l, 0], mods[:, 0], mods[:, 1])
        x = x + 0.5 * mods[:, 2][:, None, :] * swiglu(h, ffn_w13[l, 0], ffn_w2[l, 0])
        h = modulate(x, norm_w[l, 1], mods[:, 3], mods[:, 4])
        z, xbc, dt_raw, q, k, v, gt = jnp.split(h @ w_in[l], split_at, axis=-1)
        y_ssd = ssd_mixer(z, xbc, dt_raw, conv_w[l], conv_b[l], dt_bias[l], a_log[l], d_skip[l],
                          ssd_norm_w[l])
        y_ret = retention_mixer(q, k, v, gt, ret_norm_w[l], ret_norm_b[l])
        mix = jnp.concatenate([y_ssd, y_ret], axis=-1) @ w_out[l]
        x = x + mods[:, 5][:, None, :] * mix
        h = modulate(x, norm_w[l, 2], mods[:, 6], mods[:, 7])
        x = x + 0.5 * mods[:, 8][:, None, :] * swiglu(h, ffn_w13[l, 1], ffn_w2[l, 1])
    return rms_norm(x, final_norm_w)
```

```python
import functools
import math

import jax
import jax.numpy as jnp
import numpy as np
from jax import lax
from jax.experimental import pallas as pl
from jax.experimental.pallas import tpu as pltpu

F32 = jnp.float32
BF16 = jnp.bfloat16
EPS = 1e-6
LANE = 128
CHUNK = 128
SSD_HEAD_DIM = 64
SSD_GROUPS = 2
SSD_STATE = 128
SSD_CONV = 4
RET_HEAD_DIM = 128
ROPE_BASE = 10000.0
N_MOD = 9
VMEM_LIMIT = 56 * 1024 * 1024
ROW_TILE = 1024
NORM_ROWS = 64


def _silu(x):
    return x * jax.nn.sigmoid(x)


def _split3(x):
    hi = x.astype(BF16)
    r1 = x - hi.astype(F32)
    mid = r1.astype(BF16)
    lo = (r1 - mid.astype(F32)).astype(BF16)
    return hi, mid, lo


def _dot(a, b):
    return jnp.dot(a, b, preferred_element_type=F32)


def _dot_exact_rhs(m_bf16, x):
    hi, mid, lo = _split3(x)
    return _dot(m_bf16, hi) + _dot(m_bf16, mid) + _dot(m_bf16, lo)


def _dot_exact_lhs(x, m_bf16):
    hi, mid, lo = _split3(x)
    return _dot(hi, m_bf16) + _dot(mid, m_bf16) + _dot(lo, m_bf16)


def _norm_modulate_rows(x_ref, h_ref, mult, shift, rows_total):
    def body(r, carry):
        rows = pl.ds(pl.multiple_of(r * NORM_ROWS, NORM_ROWS), NORM_ROWS)
        xv = x_ref[rows, :]
        ms = jnp.mean(xv * xv, axis=-1, keepdims=True)
        h_ref[rows, :] = (xv * lax.rsqrt(ms + EPS) * mult + shift).astype(BF16)
        return carry

    lax.fori_loop(0, rows_total // NORM_ROWS, body, 0)


def _mods_kernel(c_ref, w_ref, b_ref, o_ref):
    ca = _silu(c_ref[...]).astype(BF16)
    o_ref[...] = _dot(ca, w_ref[...].astype(BF16)) + b_ref[...]


def _mods_call(c_pad, mod_w, mod_b):
    depth, d, n = mod_w.shape
    rows = c_pad.shape[0]
    tn = 1024
    return pl.pallas_call(
        _mods_kernel,
        grid=(depth, n // tn),
        in_specs=[
            pl.BlockSpec((rows, d), lambda l, j: (0, 0)),
            pl.BlockSpec((None, d, tn), lambda l, j: (l, 0, j)),
            pl.BlockSpec((None, 1, tn), lambda l, j: (l, 0, j)),
        ],
        out_specs=pl.BlockSpec((None, rows, tn), lambda l, j: (l, 0, j)),
        out_shape=jax.ShapeDtypeStruct((depth, rows, n), F32),
        compiler_params=pltpu.CompilerParams(
            dimension_semantics=("arbitrary", "arbitrary"), vmem_limit_bytes=VMEM_LIMIT),
        name="mods",
    )(c_pad, mod_w, mod_b.reshape(depth, 1, n))


def _ffn_kernel(x_ref, mod_ref, nw_ref, fw_ref, wa0_ref, wa1_ref, wb0_ref, wb1_ref, w2a_ref, w2b_ref,
                o_ref, h_ref, wcat_ref, w2c_ref, *, sub, k0, nj, has_tail, final, tm, d):
    j = pl.program_id(1)

    @pl.when(j == 0)
    def _():
        shift = mod_ref[k0:k0 + 1, :]
        scale = mod_ref[k0 + 1:k0 + 2, :]
        mult = nw_ref[sub:sub + 1, :] * (1.0 + scale)
        _norm_modulate_rows(x_ref, h_ref, mult, shift, tm)

        def zero(r, carry):
            rows = pl.ds(pl.multiple_of(r * NORM_ROWS, NORM_ROWS), NORM_ROWS)
            o_ref[rows, :] = jnp.zeros((NORM_ROWS, d), F32)
            return carry

        lax.fori_loop(0, tm // NORM_ROWS, zero, 0)

    wcat_ref[:, 0 * LANE:1 * LANE] = wa0_ref[...].astype(BF16)
    wcat_ref[:, 1 * LANE:2 * LANE] = wa1_ref[...].astype(BF16)
    wcat_ref[:, 2 * LANE:3 * LANE] = wb0_ref[...].astype(BF16)
    wcat_ref[:, 3 * LANE:4 * LANE] = wb1_ref[...].astype(BF16)
    ab = _dot(h_ref[...], wcat_ref[...])
    a = ab[:, :2 * LANE]
    b = ab[:, 2 * LANE:]
    g = _silu(a) * b
    w2_hi = w2b_ref[...]
    if has_tail:
        live = j < nj - 1
        col = lax.broadcasted_iota(jnp.int32, g.shape, 1)
        g = jnp.where((col < LANE) | live, g, 0.0)
        w2_hi = jnp.where(live, w2_hi, 0.0)
    gb = g.astype(BF16)
    w2c_ref[0:LANE, :] = w2a_ref[...].astype(BF16)
    w2c_ref[LANE:2 * LANE, :] = w2_hi.astype(BF16)
    nc = 512
    for n0 in range(0, d, nc):
        o_ref[:, n0:n0 + nc] += _dot(gb, w2c_ref[:, n0:n0 + nc])

    @pl.when(j == nj - 1)
    def _():
        gate = 0.5 * mod_ref[k0 + 2:k0 + 3, :]
        fw = fw_ref[...]

        def fin(r, carry):
            rows = pl.ds(pl.multiple_of(r * NORM_ROWS, NORM_ROWS), NORM_ROWS)
            xn = x_ref[rows, :] + gate * o_ref[rows, :]
            if final:
                ms = jnp.mean(xn * xn, axis=-1, keepdims=True)
                xn = xn * lax.rsqrt(ms + EPS) * fw
            o_ref[rows, :] = xn
            return carry

        lax.fori_loop(0, tm // NORM_ROWS, fin, 0)


def _ffn_call(x2, mods4, norm_w, final_w, w13, w2, *, layer, half, sub, k0, final, tiles_per_batch, tm):
    m, d = x2.shape
    d_ff = w2.shape[2]
    nblk = d_ff // LANE
    nj = (nblk + 1) // 2
    has_tail = nblk % 2 == 1
    last = nblk - 1
    l, s = layer, half
    kern = functools.partial(_ffn_kernel, sub=sub, k0=k0, nj=nj, has_tail=has_tail, final=final, tm=tm, d=d)
    once = pl.Buffered(1)
    return pl.pallas_call(
        kern,
        grid=(m // tm, nj),
        in_specs=[
            pl.BlockSpec((tm, d), lambda i, j: (i, 0), pipeline_mode=once),
            pl.BlockSpec((None, None, N_MOD, d), lambda i, j: (l, i // tiles_per_batch, 0, 0)),
            pl.BlockSpec((None, 3, d), lambda i, j: (l, 0, 0)),
            pl.BlockSpec((1, d), lambda i, j: (0, 0)),
            pl.BlockSpec((None, None, d, LANE), lambda i, j: (l, s, 0, 2 * j)),
            pl.BlockSpec((None, None, d, LANE), lambda i, j: (l, s, 0, jnp.minimum(2 * j + 1, last))),
            pl.BlockSpec((None, None, d, LANE), lambda i, j: (l, s, 0, nblk + 2 * j)),
            pl.BlockSpec((None, None, d, LANE), lambda i, j: (l, s, 0, nblk + jnp.minimum(2 * j + 1, last))),
            pl.BlockSpec((None, None, LANE, d), lambda i, j: (l, s, 2 * j, 0)),
            pl.BlockSpec((None, None, LANE, d), lambda i, j: (l, s, jnp.minimum(2 * j + 1, last), 0)),
        ],
        out_specs=pl.BlockSpec((tm, d), lambda i, j: (i, 0), pipeline_mode=once),
        out_shape=jax.ShapeDtypeStruct((m, d), F32),
        scratch_shapes=[
            pltpu.VMEM((tm, d), BF16),
            pltpu.VMEM((d, 4 * LANE), BF16),
            pltpu.VMEM((2 * LANE, d), BF16),
        ],
        compiler_params=pltpu.CompilerParams(
            dimension_semantics=("arbitrary", "arbitrary"), vmem_limit_bytes=VMEM_LIMIT),
        name=f"ffn_{sub}",
    )(x2, mods4, norm_w, final_w, w13, w13, w13, w13, w2, w2)


def _inproj_kernel(x_ref, mod_ref, nw_ref, w_ref, wdt_ref, o_ref, dt_ref, h_ref, *, tm):
    j = pl.program_id(1)

    @pl.when(j == 0)
    def _():
        shift = mod_ref[3:4, :]
        scale = mod_ref[4:5, :]
        mult = nw_ref[1:2, :] * (1.0 + scale)
        _norm_modulate_rows(x_ref, h_ref, mult, shift, tm)
        dt_ref[...] = _dot(h_ref[...], wdt_ref[...])

    o_ref[...] = _dot(h_ref[...], w_ref[...]).astype(BF16)


def _inproj_call(x2, mods4, norm_w, w_main, w_dt, *, layer, tiles_per_batch, tm):
    m, d = x2.shape
    n = w_main.shape[2]
    tn = 512
    l = layer
    return pl.pallas_call(
        functools.partial(_inproj_kernel, tm=tm),
        grid=(m // tm, n // tn),
        in_specs=[
            pl.BlockSpec((tm, d), lambda i, j: (i, 0), pipeline_mode=pl.Buffered(1)),
            pl.BlockSpec((None, None, N_MOD, d), lambda i, j: (l, i // tiles_per_batch, 0, 0)),
            pl.BlockSpec((None, 3, d), lambda i, j: (l, 0, 0)),
            pl.BlockSpec((None, d, tn), lambda i, j: (l, 0, j)),
            pl.BlockSpec((None, d, LANE), lambda i, j: (l, 0, 0)),
        ],
        out_specs=[
            pl.BlockSpec((tm, tn), lambda i, j: (i, j)),
            pl.BlockSpec((tm, LANE), lambda i, j: (i, 0)),
        ],
        out_shape=[
            jax.ShapeDtypeStruct((m, n), BF16),
            jax.ShapeDtypeStruct((m, LANE), F32),
        ],
        scratch_shapes=[pltpu.VMEM((tm, d), BF16)],
        compiler_params=pltpu.CompilerParams(
            dimension_semantics=("arbitrary", "arbitrary"), vmem_limit_bytes=VMEM_LIMIT),
        name="inproj",
    )(x2, mods4, norm_w, w_main, w_dt)


def _ssd_kernel(z_ref, xs_ref, bc_ref, dt_ref, cw_ref, cb_ref, dtb_ref, alog_ref, dskip_ref, nw_ref,
                o_ref, ext_ref, state_ref, y_ref, *, n_heads, d_ssd):
    c = pl.program_id(1)
    gw = SSD_GROUPS * SSD_STATE
    hpg = n_heads // SSD_GROUPS
    gch = hpg * SSD_HEAD_DIM

    @pl.when(c == 0)
    def _():
        ext_ref[0:8, :] = jnp.zeros((8, ext_ref.shape[1]), F32)
        state_ref[...] = jnp.zeros(state_ref.shape, F32)

    ext_ref[8:8 + CHUNK, 0:d_ssd] = xs_ref[...].astype(F32)
    ext_ref[8:8 + CHUNK, d_ssd:] = bc_ref[...].astype(F32)
    conv = cb_ref[...] + cw_ref[0:1, :] * ext_ref[pl.ds(8 - SSD_CONV + 1, CHUNK), :]
    for k in range(1, SSD_CONV):
        conv = conv + cw_ref[k:k + 1, :] * ext_ref[pl.ds(8 - SSD_CONV + 1 + k, CHUNK), :]
    ext_ref[0:8, :] = ext_ref[CHUNK:CHUNK + 8, :]
    act = _silu(conv)
    xs_a = act[:, :d_ssd]
    bm = act[:, d_ssd:d_ssd + gw]
    cm = act[:, d_ssd + gw:]

    lane = lax.broadcasted_iota(jnp.int32, (CHUNK, LANE), 1)
    row = lax.broadcasted_iota(jnp.int32, (CHUNK, LANE), 0)
    head_lane = lane < n_heads
    dt_in = dt_ref[...] + dtb_ref[...]
    dt = jnp.maximum(dt_in, 0.0) + jnp.log1p(jnp.exp(-jnp.abs(dt_in)))
    dt = jnp.where(head_lane, dt, 0.0)
    a = dt * (-jnp.exp(alog_ref[...]))
    causal = row >= lane
    tri = jnp.where(causal, 1.0, 0.0).astype(BF16)
    a_cum = _dot_exact_rhs(tri, a)
    a_cum_t = a_cum.T
    ea = jnp.exp(a_cum)
    dec = jnp.exp(a_cum[CHUNK - 1:CHUNK, :] - a_cum)

    erow = lax.broadcasted_iota(jnp.int32, (LANE, d_ssd), 0)
    ecol = lax.broadcasted_iota(jnp.int32, (LANE, d_ssd), 1)
    expand = jnp.where(ecol // SSD_HEAD_DIM == erow, 1.0, 0.0).astype(BF16)
    dt_x = _dot_exact_lhs(dt, expand)
    dtdec_x = _dot_exact_lhs(dt * dec, expand)
    ea_x = _dot_exact_lhs(ea, expand)
    xdt_b = (xs_a * dt_x).astype(BF16)
    xsd_b = (xs_a * dtdec_x).astype(BF16)
    dskip = dskip_ref[...]

    lo_half = lane < SSD_HEAD_DIM
    pairs = hpg // 2
    for g in range(SSD_GROUPS):
        bg = bm[:, g * SSD_STATE:(g + 1) * SSD_STATE].astype(BF16)
        cg = cm[:, g * SSD_STATE:(g + 1) * SSD_STATE].astype(BF16)
        cb = lax.dot_general(cg, bg, (((1,), (1,)), ((), ())), preferred_element_type=F32)
        gsl = slice(g * gch, (g + 1) * gch)
        prev_t = state_ref[g]
        y_off = _dot(cg, prev_t.astype(BF16)) * ea_x[:, gsl]
        states_t = lax.dot_general(bg, xsd_b[:, gsl], (((0,), (0,)), ((), ())),
                                   preferred_element_type=F32)
        state_ref[g] = prev_t * ea_x[CHUNK - 1:CHUNK, gsl] + states_t
        for p in range(pairs):
            t = g * pairs + p
            tsl = slice(t * LANE, (t + 1) * LANE)
            xt = xdt_b[:, tsl]
            yd = None
            for q in range(2):
                h = 2 * t + q
                seg = a_cum[:, h:h + 1] - a_cum_t[h:h + 1, :]
                lmat = jnp.exp(jnp.where(causal, seg, -jnp.inf))
                mh = (cb * lmat).astype(BF16)
                xh = jnp.where(lo_half if q == 0 else jnp.logical_not(lo_half), xt, jnp.zeros_like(xt))
                part = _dot(mh, xh)
                yd = part if yd is None else yd + part
            y_ref[:, tsl] = yd + y_off[:, p * LANE:(p + 1) * LANE] + xs_a[:, tsl] * dskip[:, tsl]

    yz = y_ref[...] * _silu(z_ref[...].astype(F32))
    ms = jnp.mean(yz * yz, axis=-1, keepdims=True)
    o_ref[...] = (yz * lax.rsqrt(ms + EPS) * nw_ref[...]).astype(BF16)


def _ssd_call(proj, dt_raw, conv_w, conv_b, dt_bias_p, a_log_p, d_skip_x, ssd_norm_w, *, layer, batch, seq,
              d_ssd, n_heads):
    nc = seq // CHUNK
    l = layer
    xbc_w = conv_w.shape[2]
    bc_w = xbc_w - d_ssd
    bc_blk = (6 * d_ssd) // bc_w
    depth = conv_w.shape[0]
    hpg = n_heads // SSD_GROUPS
    return pl.pallas_call(
        functools.partial(_ssd_kernel, n_heads=n_heads, d_ssd=d_ssd),
        grid=(batch, nc),
        in_specs=[
            pl.BlockSpec((CHUNK, d_ssd), lambda b, c: (b * nc + c, 0)),
            pl.BlockSpec((CHUNK, d_ssd), lambda b, c: (b * nc + c, 1)),
            pl.BlockSpec((CHUNK, bc_w), lambda b, c: (b * nc + c, bc_blk)),
            pl.BlockSpec((CHUNK, LANE), lambda b, c: (b * nc + c, 0)),
            pl.BlockSpec((None, SSD_CONV, xbc_w), lambda b, c: (l, 0, 0)),
            pl.BlockSpec((None, 1, xbc_w), lambda b, c: (l, 0, 0)),
            pl.BlockSpec((None, 1, LANE), lambda b, c: (l, 0, 0)),
            pl.BlockSpec((None, 1, LANE), lambda b, c: (l, 0, 0)),
            pl.BlockSpec((None, 1, d_ssd), lambda b, c: (l, 0, 0)),
            pl.BlockSpec((None, 1, d_ssd), lambda b, c: (l, 0, 0)),
        ],
        out_specs=pl.BlockSpec((CHUNK, d_ssd), lambda b, c: (b * nc + c, 0)),
        out_shape=jax.ShapeDtypeStruct((batch * seq, d_ssd), BF16),
        scratch_shapes=[
            pltpu.VMEM((CHUNK + 8, xbc_w), F32),
            pltpu.VMEM((SSD_GROUPS, SSD_STATE, hpg * SSD_HEAD_DIM), F32),
            pltpu.VMEM((CHUNK, d_ssd), F32),
        ],
        compiler_params=pltpu.CompilerParams(
            dimension_semantics=("arbitrary", "arbitrary"), vmem_limit_bytes=VMEM_LIMIT),
        name="ssd",
    )(proj, proj, proj, dt_raw, conv_w, conv_b.reshape(depth, 1, xbc_w), dt_bias_p, a_log_p, d_skip_x,
      ssd_norm_w.reshape(depth, 1, d_ssd))


def _ret_kernel(q_ref, k_ref, v_ref, g_ref, cos_ref, sin_ref, gw_ref, gb_ref, o_ref, state_ref, dec_ref,
                *, n_heads, log_g):
    b = pl.program_id(0)
    c = pl.program_id(1)
    dh = RET_HEAD_DIM
    row = lax.broadcasted_iota(jnp.int32, (CHUNK, CHUNK), 0)
    col = lax.broadcasted_iota(jnp.int32, (CHUNK, CHUNK), 1)

    @pl.when((b == 0) & (c == 0))
    def _():
        diff = (row - col).astype(F32)
        for h in range(n_heads):
            dec_ref[h] = jnp.where(diff >= 0, jnp.exp(jnp.maximum(diff, 0.0) * log_g[h]), 0.0)

    @pl.when(c == 0)
    def _():
        state_ref[...] = jnp.zeros(state_ref.shape, F32)

    cosf = cos_ref[...]
    sinf = sin_ref[...]
    pos = lax.broadcasted_iota(jnp.int32, (CHUNK, 1), 0).astype(F32)
    k_scale = dh ** -0.5
    for h in range(n_heads):
        hs = slice(h * dh, (h + 1) * dh)
        qh = q_ref[:, hs].astype(F32)
        kh = k_ref[:, hs].astype(F32)
        qr = qh * cosf + pltpu.roll(qh, dh // 2, 1) * sinf
        kr = (kh * cosf + pltpu.roll(kh, dh // 2, 1) * sinf) * k_scale
        vh = v_ref[:, hs]
        qb = qr.astype(BF16)
        kb = kr.astype(BF16)
        scores = lax.dot_general(qb, kb, (((1,), (1,)), ((), ())), preferred_element_type=F32)
        inner = _dot((scores * dec_ref[h]).astype(BF16), vh)
        k_dec = jnp.exp((CHUNK - 1.0 - pos) * log_g[h])
        q_dec = jnp.exp((pos + 1.0) * log_g[h])
        prev = state_ref[h]
        cross = _dot((qr * q_dec).astype(BF16), prev.astype(BF16))
        kv = lax.dot_general((kr * k_dec).astype(BF16), vh, (((0,), (0,)), ((), ())),
                             preferred_element_type=F32)
        state_ref[h] = prev * math.exp(CHUNK * log_g[h]) + kv
        o = inner + cross
        mu = jnp.mean(o, axis=-1, keepdims=True)
        oc = o - mu
        var = jnp.mean(oc * oc, axis=-1, keepdims=True)
        on = oc * lax.rsqrt(var + EPS) * gw_ref[:, hs] + gb_ref[:, hs]
        o_ref[:, hs] = (_silu(g_ref[:, hs].astype(F32)) * on).astype(BF16)


def _ret_call(proj, cos_t, sin_t, gn_w, gn_b, *, layer, batch, seq, d_ret, n_heads):
    nc = seq // CHUNK
    l = layer
    depth = gn_w.shape[0]
    log_g = tuple(math.log(1.0 - 2.0 ** (-5.0 - h)) for h in range(n_heads))
    return pl.pallas_call(
        functools.partial(_ret_kernel, n_heads=n_heads, log_g=log_g),
        grid=(batch, nc),
        in_specs=[
            pl.BlockSpec((CHUNK, d_ret), lambda b, c: (b * nc + c, 2)),
            pl.BlockSpec((CHUNK, d_ret), lambda b, c: (b * nc + c, 3)),
            pl.BlockSpec((CHUNK, d_ret), lambda b, c: (b * nc + c, 4)),
            pl.BlockSpec((CHUNK, d_ret), lambda b, c: (b * nc + c, 5)),
            pl.BlockSpec((CHUNK, RET_HEAD_DIM), lambda b, c: (c, 0)),
            pl.BlockSpec((CHUNK, RET_HEAD_DIM), lambda b, c: (c, 0)),
            pl.BlockSpec((None, 1, d_ret), lambda b, c: (l, 0, 0)),
            pl.BlockSpec((None, 1, d_ret), lambda b, c: (l, 0, 0)),
        ],
        out_specs=pl.BlockSpec((CHUNK, d_ret), lambda b, c: (b * nc + c, 0)),
        out_shape=jax.ShapeDtypeStruct((batch * seq, d_ret), BF16),
        scratch_shapes=[
            pltpu.VMEM((n_heads, RET_HEAD_DIM, RET_HEAD_DIM), F32),
            pltpu.VMEM((n_heads, CHUNK, CHUNK), F32),
        ],
        compiler_params=pltpu.CompilerParams(
            dimension_semantics=("arbitrary", "arbitrary"), vmem_limit_bytes=VMEM_LIMIT),
        name="retention",
    )(proj, proj, proj, proj, cos_t, sin_t, gn_w.reshape(depth, 1, d_ret), gn_b.reshape(depth, 1, d_ret))


def _outproj_kernel(x_ref, mod_ref, ys_ref, yr_ref, w_ref, o_ref, *, d_ssd):
    w = w_ref[...]
    acc = _dot(ys_ref[...], w[:d_ssd].astype(BF16)) + _dot(yr_ref[...], w[d_ssd:].astype(BF16))
    o_ref[...] = x_ref[...] + mod_ref[5:6, :] * acc


def _outproj_call(x2, mods4, y_ssd, y_ret, w_out, *, layer, tiles_per_batch, tm):
    m, d = x2.shape
    d_ssd = y_ssd.shape[1]
    d_ret = y_ret.shape[1]
    tn = 512
    l = layer
    return pl.pallas_call(
        functools.partial(_outproj_kernel, d_ssd=d_ssd),
        grid=(m // tm, d // tn),
        in_specs=[
            pl.BlockSpec((tm, tn), lambda i, j: (i, j)),
            pl.BlockSpec((None, None, N_MOD, tn), lambda i, j: (l, i // tiles_per_batch, 0, j)),
            pl.BlockSpec((tm, d_ssd), lambda i, j: (i, 0)),
            pl.BlockSpec((tm, d_ret), lambda i, j: (i, 0)),
            pl.BlockSpec((None, d_ssd + d_ret, tn), lambda i, j: (l, 0, j)),
        ],
        out_specs=pl.BlockSpec((tm, tn), lambda i, j: (i, j)),
        out_shape=jax.ShapeDtypeStruct((m, d), F32),
        compiler_params=pltpu.CompilerParams(
            dimension_semantics=("arbitrary", "arbitrary"), vmem_limit_bytes=VMEM_LIMIT),
        name="outproj",
    )(x2, mods4, y_ssd, y_ret, w_out)


def _pad_lanes(v):
    return jnp.pad(v, ((0, 0), (0, LANE - v.shape[1])))[:, None, :]


def kernel(x, c, norm_w, mod_w, mod_b, ffn_w13, ffn_w2, w_in, conv_w, conv_b, dt_bias, a_log, d_skip,
           ssd_norm_w, ret_norm_w, ret_norm_b, w_out, final_norm_w):
    batch, seq, d = x.shape
    depth = norm_w.shape[0]
    n_heads = dt_bias.shape[1]
    d_ssd = ssd_norm_w.shape[1]
    d_ret = ret_norm_w.shape[1]
    ret_heads = d_ret // RET_HEAD_DIM
    xbc_w = conv_w.shape[2]
    m = batch * seq
    tm = min(ROW_TILE, seq)
    tiles_per_batch = seq // tm

    pad_rows = 16
    c_pad = jnp.pad(c, ((0, pad_rows - batch), (0, 0)))
    mods = _mods_call(c_pad, mod_w, mod_b)[:, :batch].reshape(depth, batch, N_MOD, d)

    o_z, o_xbc, o_dt = 0, d_ssd, d_ssd + xbc_w
    o_q = o_dt + n_heads
    w_main = jnp.concatenate(
        [w_in[:, :, o_z:o_z + 2 * d_ssd], w_in[:, :, o_q:o_q + 4 * d_ret], w_in[:, :, o_xbc + d_ssd:o_dt]],
        axis=-1).astype(BF16)
    w_dt = jnp.pad(w_in[:, :, o_dt:o_q], ((0, 0), (0, 0), (0, LANE - n_heads))).astype(BF16)

    half = RET_HEAD_DIM // 2
    inv = ROPE_BASE ** (-jnp.arange(0, RET_HEAD_DIM, 2, dtype=F32) / RET_HEAD_DIM)
    ang = jnp.arange(seq, dtype=F32)[:, None] * inv[None, :]
    cos_t = jnp.concatenate([jnp.cos(ang), jnp.cos(ang)], axis=-1)
    sin_t = jnp.concatenate([-jnp.sin(ang), jnp.sin(ang)], axis=-1)
    assert cos_t.shape == (seq, 2 * half)

    dt_bias_p = _pad_lanes(dt_bias)
    a_log_p = _pad_lanes(a_log)
    d_skip_x = jnp.repeat(d_skip, SSD_HEAD_DIM, axis=1)[:, None, :]
    final_w = final_norm_w.reshape(1, d)

    x2 = x.reshape(m, d)
    common = dict(tiles_per_batch=tiles_per_batch, tm=tm)
    for l in range(depth):
        x2 = _ffn_call(x2, mods, norm_w, final_w, ffn_w13, ffn_w2, layer=l, half=0, sub=0, k0=0,
                       final=False, **common)
        proj, dt_raw = _inproj_call(x2, mods, norm_w, w_main, w_dt, layer=l, **common)
        y_ssd = _ssd_call(proj, dt_raw, conv_w, conv_b, dt_bias_p, a_log_p, d_skip_x, ssd_norm_w, layer=l,
                          batch=batch, seq=seq, d_ssd=d_ssd, n_heads=n_heads)
        y_ret = _ret_call(proj, cos_t, sin_t, ret_norm_w, ret_norm_b, layer=l, batch=batch, seq=seq,
                          d_ret=d_ret, n_heads=ret_heads)
        x2 = _outproj_call(x2, mods, y_ssd, y_ret, w_out, layer=l, **common)
        x2 = _ffn_call(x2, mods, norm_w, final_w, ffn_w13, ffn_w2, layer=l, half=1, sub=2, k0=6,
                       final=(l == depth - 1), **common)
    return x2.reshape(batch, seq, d)
```

```python
import functools
import math

import jax
import jax.numpy as jnp
import numpy as np
from jax import lax
from jax.experimental import pallas as pl
from jax.experimental.pallas import tpu as pltpu

F32 = jnp.float32
BF16 = jnp.bfloat16
EPS = 1e-6
LANE = 128
CHUNK = 128
SSD_HEAD_DIM = 64
SSD_GROUPS = 2
SSD_STATE = 128
SSD_CONV = 4
RET_HEAD_DIM = 128
ROPE_BASE = 10000.0
N_MOD = 9
VMEM_LIMIT = 56 * 1024 * 1024
ROW_TILE = 1024
NORM_ROWS = 64


def _silu(x):
    return x * jax.nn.sigmoid(x)


def _split3(x):
    hi = x.astype(BF16)
    r1 = x - hi.astype(F32)
    mid = r1.astype(BF16)
    lo = (r1 - mid.astype(F32)).astype(BF16)
    return hi, mid, lo


def _dot(a, b):
    return jnp.dot(a, b, preferred_element_type=F32)


def _dot_exact_rhs(m_bf16, x):
    hi, mid, lo = _split3(x)
    return _dot(m_bf16, hi) + _dot(m_bf16, mid) + _dot(m_bf16, lo)


def _dot_exact_lhs(x, m_bf16):
    hi, mid, lo = _split3(x)
    return _dot(hi, m_bf16) + _dot(mid, m_bf16) + _dot(lo, m_bf16)


def _norm_modulate_rows(x_ref, h_ref, mult, shift, rows_total, zero_ref=None):
    def body(r, carry):
        rows = pl.ds(pl.multiple_of(r * NORM_ROWS, NORM_ROWS), NORM_ROWS)
        xv = x_ref[rows, :]
        ms = jnp.mean(xv * xv, axis=-1, keepdims=True)
        h_ref[rows, :] = (xv * lax.rsqrt(ms + EPS) * mult + shift).astype(BF16)
        if zero_ref is not None:
            zero_ref[rows, :] = jnp.zeros((NORM_ROWS, zero_ref.shape[1]), zero_ref.dtype)
        return carry

    lax.fori_loop(0, rows_total // NORM_ROWS, body, 0)


def _mods_kernel(c_ref, w_ref, b_ref, o_ref):
    ca = _silu(c_ref[...]).astype(BF16)
    o_ref[...] = _dot(ca, w_ref[...].astype(BF16)) + b_ref[...]


def _mods_call(c_pad, mod_w, mod_b):
    depth, d, n = mod_w.shape
    rows = c_pad.shape[0]
    tn = 1024
    return pl.pallas_call(
        _mods_kernel,
        grid=(depth, n // tn),
        in_specs=[
            pl.BlockSpec((rows, d), lambda l, j: (0, 0)),
            pl.BlockSpec((None, d, tn), lambda l, j: (l, 0, j)),
            pl.BlockSpec((None, 1, tn), lambda l, j: (l, 0, j)),
        ],
        out_specs=pl.BlockSpec((None, rows, tn), lambda l, j: (l, 0, j)),
        out_shape=jax.ShapeDtypeStruct((depth, rows, n), F32),
        compiler_params=pltpu.CompilerParams(
            dimension_semantics=("arbitrary", "arbitrary"), vmem_limit_bytes=VMEM_LIMIT),
        name="mods",
    )(c_pad, mod_w, mod_b.reshape(depth, 1, n))


def _ffn_kernel(x_ref, mod_ref, nw_ref, fw_ref, wa0_ref, wa1_ref, wb0_ref, wb1_ref, w2a_ref, w2b_ref,
                o_ref, h_ref, wcat_ref, w2c_ref, *, sub, k0, nj, has_tail, final, tm, d):
    j = pl.program_id(1)

    @pl.when(j == 0)
    def _():
        shift = mod_ref[k0:k0 + 1, :]
        scale = mod_ref[k0 + 1:k0 + 2, :]
        mult = nw_ref[sub:sub + 1, :] * (1.0 + scale)
        _norm_modulate_rows(x_ref, h_ref, mult, shift, tm, zero_ref=o_ref)

    wcat_ref[:, 0 * LANE:1 * LANE] = wa0_ref[...].astype(BF16)
    wcat_ref[:, 1 * LANE:2 * LANE] = wa1_ref[...].astype(BF16)
    wcat_ref[:, 2 * LANE:3 * LANE] = wb0_ref[...].astype(BF16)
    wcat_ref[:, 3 * LANE:4 * LANE] = wb1_ref[...].astype(BF16)
    ab = _dot(h_ref[...], wcat_ref[...])
    a = ab[:, :2 * LANE]
    b = ab[:, 2 * LANE:]
    g = _silu(a) * b
    w2_hi = w2b_ref[...]
    if has_tail:
        live = j < nj - 1
        col = lax.broadcasted_iota(jnp.int32, g.shape, 1)
        g = jnp.where((col < LANE) | live, g, 0.0)
        w2_hi = jnp.where(live, w2_hi, 0.0)
    gb = g.astype(BF16)
    w2c_ref[0:LANE, :] = w2a_ref[...].astype(BF16)
    w2c_ref[LANE:2 * LANE, :] = w2_hi.astype(BF16)
    nc = 512
    for n0 in range(0, d, nc):
        o_ref[:, n0:n0 + nc] += _dot(gb, w2c_ref[:, n0:n0 + nc])

    @pl.when(j == nj - 1)
    def _():
        gate = 0.5 * mod_ref[k0 + 2:k0 + 3, :]
        fw = fw_ref[...]

        def fin(r, carry):
            rows = pl.ds(pl.multiple_of(r * NORM_ROWS, NORM_ROWS), NORM_ROWS)
            xn = x_ref[rows, :] + gate * o_ref[rows, :]
            if final:
                ms = jnp.mean(xn * xn, axis=-1, keepdims=True)
                xn = xn * lax.rsqrt(ms + EPS) * fw
            o_ref[rows, :] = xn
            return carry

        lax.fori_loop(0, tm // NORM_ROWS, fin, 0)


def _ffn_call(x2, mods4, norm_w, final_w, w13, w2, *, layer, half, sub, k0, final, tiles_per_batch, tm):
    m, d = x2.shape
    d_ff = w2.shape[2]
    nblk = d_ff // LANE
    nj = (nblk + 1) // 2
    has_tail = nblk % 2 == 1
    last = nblk - 1
    l, s = layer, half
    kern = functools.partial(_ffn_kernel, sub=sub, k0=k0, nj=nj, has_tail=has_tail, final=final, tm=tm, d=d)
    return pl.pallas_call(
        kern,
        grid=(m // tm, nj),
        in_specs=[
            pl.BlockSpec((tm, d), lambda i, j: (i, 0)),
            pl.BlockSpec((None, None, N_MOD, d), lambda i, j: (l, i // tiles_per_batch, 0, 0)),
            pl.BlockSpec((None, 3, d), lambda i, j: (l, 0, 0)),
            pl.BlockSpec((1, d), lambda i, j: (0, 0)),
            pl.BlockSpec((None, None, d, LANE), lambda i, j: (l, s, 0, 2 * j)),
            pl.BlockSpec((None, None, d, LANE), lambda i, j: (l, s, 0, jnp.minimum(2 * j + 1, last))),
            pl.BlockSpec((None, None, d, LANE), lambda i, j: (l, s, 0, nblk + 2 * j)),
            pl.BlockSpec((None, None, d, LANE), lambda i, j: (l, s, 0, nblk + jnp.minimum(2 * j + 1, last))),
            pl.BlockSpec((None, None, LANE, d), lambda i, j: (l, s, 2 * j, 0)),
            pl.BlockSpec((None, None, LANE, d), lambda i, j: (l, s, jnp.minimum(2 * j + 1, last), 0)),
        ],
        out_specs=pl.BlockSpec((tm, d), lambda i, j: (i, 0)),
        out_shape=jax.ShapeDtypeStruct((m, d), F32),
        scratch_shapes=[
            pltpu.VMEM((tm, d), BF16),
            pltpu.VMEM((d, 4 * LANE), BF16),
            pltpu.VMEM((2 * LANE, d), BF16),
        ],
        compiler_params=pltpu.CompilerParams(
            dimension_semantics=("arbitrary", "arbitrary"), vmem_limit_bytes=VMEM_LIMIT),
        name=f"ffn_{sub}",
    )(x2, mods4, norm_w, final_w, w13, w13, w13, w13, w2, w2)


def _inproj_kernel(x_ref, mod_ref, nw_ref, w_ref, wdt_ref, o_ref, dt_ref, h_ref, *, tm, tn):
    shift = mod_ref[3:4, :]
    scale = mod_ref[4:5, :]
    mult = nw_ref[1:2, :] * (1.0 + scale)
    _norm_modulate_rows(x_ref, h_ref, mult, shift, tm)
    h = h_ref[...]
    dt_ref[...] = _dot(h, wdt_ref[...])
    for n0 in range(0, o_ref.shape[1], tn):
        o_ref[:, n0:n0 + tn] = _dot(h, w_ref[:, n0:n0 + tn]).astype(BF16)


def _inproj_call(x2, mods4, norm_w, w_main, w_dt, *, layer, tiles_per_batch, tm):
    m, d = x2.shape
    n = w_main.shape[2]
    tr = tm // 2
    tpb = tiles_per_batch * 2
    l = layer
    once = pl.Buffered(1)
    return pl.pallas_call(
        functools.partial(_inproj_kernel, tm=tr, tn=512),
        grid=(m // tr,),
        in_specs=[
            pl.BlockSpec((tr, d), lambda i: (i, 0)),
            pl.BlockSpec((None, None, N_MOD, d), lambda i: (l, i // tpb, 0, 0)),
            pl.BlockSpec((None, 3, d), lambda i: (l, 0, 0)),
            pl.BlockSpec((None, d, n), lambda i: (l, 0, 0), pipeline_mode=once),
            pl.BlockSpec((None, d, LANE), lambda i: (l, 0, 0), pipeline_mode=once),
        ],
        out_specs=[
            pl.BlockSpec((tr, n), lambda i: (i, 0)),
            pl.BlockSpec((tr, LANE), lambda i: (i, 0)),
        ],
        out_shape=[
            jax.ShapeDtypeStruct((m, n), BF16),
            jax.ShapeDtypeStruct((m, LANE), F32),
        ],
        scratch_shapes=[pltpu.VMEM((tr, d), BF16)],
        compiler_params=pltpu.CompilerParams(
            dimension_semantics=("arbitrary",), vmem_limit_bytes=VMEM_LIMIT),
        name="inproj",
    )(x2, mods4, norm_w, w_main, w_dt)


def _ssd_kernel(z_ref, xs_ref, bc_ref, dt_ref, cw_ref, cb_ref, dtb_ref, alog_ref, dskip_ref, nw_ref,
                o_ref, ext_ref, state_ref, y_ref, *, n_heads, d_ssd):
    c = pl.program_id(1)
    gw = SSD_GROUPS * SSD_STATE
    hpg = n_heads // SSD_GROUPS
    gch = hpg * SSD_HEAD_DIM

    @pl.when(c == 0)
    def _():
        ext_ref[0:8, :] = jnp.zeros((8, ext_ref.shape[1]), F32)
        state_ref[...] = jnp.zeros(state_ref.shape, F32)

    ext_ref[8:8 + CHUNK, 0:d_ssd] = xs_ref[...].astype(F32)
    ext_ref[8:8 + CHUNK, d_ssd:] = bc_ref[...].astype(F32)
    conv = cb_ref[...] + cw_ref[0:1, :] * ext_ref[pl.ds(8 - SSD_CONV + 1, CHUNK), :]
    for k in range(1, SSD_CONV):
        conv = conv + cw_ref[k:k + 1, :] * ext_ref[pl.ds(8 - SSD_CONV + 1 + k, CHUNK), :]
    ext_ref[0:8, :] = ext_ref[CHUNK:CHUNK + 8, :]
    act = _silu(conv)
    xs_a = act[:, :d_ssd]
    bm = act[:, d_ssd:d_ssd + gw]
    cm = act[:, d_ssd + gw:]

    lane = lax.broadcasted_iota(jnp.int32, (CHUNK, LANE), 1)
    row = lax.broadcasted_iota(jnp.int32, (CHUNK, LANE), 0)
    head_lane = lane < n_heads
    dt_in = dt_ref[...] + dtb_ref[...]
    dt = jnp.maximum(dt_in, 0.0) + jnp.log1p(jnp.exp(-jnp.abs(dt_in)))
    dt = jnp.where(head_lane, dt, 0.0)
    a = dt * (-jnp.exp(alog_ref[...]))
    causal = row >= lane
    tri = jnp.where(causal, 1.0, 0.0).astype(BF16)
    a_cum = _dot_exact_rhs(tri, a)
    a_cum_t = a_cum.T
    ea = jnp.exp(a_cum)
    dec = jnp.exp(a_cum[CHUNK - 1:CHUNK, :] - a_cum)

    erow = lax.broadcasted_iota(jnp.int32, (LANE, d_ssd), 0)
    ecol = lax.broadcasted_iota(jnp.int32, (LANE, d_ssd), 1)
    expand = jnp.where(ecol // SSD_HEAD_DIM == erow, 1.0, 0.0).astype(BF16)
    dt_x = _dot_exact_lhs(dt, expand)
    dtdec_x = _dot_exact_lhs(dt * dec, expand)
    ea_x = _dot_exact_lhs(ea, expand)
    xdt_b = (xs_a * dt_x).astype(BF16)
    xsd_b = (xs_a * dtdec_x).astype(BF16)
    dskip = dskip_ref[...]

    lo_half = lane < SSD_HEAD_DIM
    pairs = hpg // 2
    for g in range(SSD_GROUPS):
        bg = bm[:, g * SSD_STATE:(g + 1) * SSD_STATE].astype(BF16)
        cg = cm[:, g * SSD_STATE:(g + 1) * SSD_STATE].astype(BF16)
        cb = lax.dot_general(cg, bg, (((1,), (1,)), ((), ())), preferred_element_type=F32)
        gsl = slice(g * gch, (g + 1) * gch)
        prev_t = state_ref[g]
        y_off = _dot(cg, prev_t.astype(BF16)) * ea_x[:, gsl]
        states_t = lax.dot_general(bg, xsd_b[:, gsl], (((0,), (0,)), ((), ())),
                                   preferred_element_type=F32)
        state_ref[g] = prev_t * ea_x[CHUNK - 1:CHUNK, gsl] + states_t
        for p in range(pairs):
            t = g * pairs + p
            tsl = slice(t * LANE, (t + 1) * LANE)
            xt = xdt_b[:, tsl]
            yd = None
            for q in range(2):
                h = 2 * t + q
                seg = a_cum[:, h:h + 1] - a_cum_t[h:h + 1, :]
                lmat = jnp.exp(jnp.where(causal, seg, -jnp.inf))
                mh = (cb * lmat).astype(BF16)
                xh = jnp.where(lo_half if q == 0 else jnp.logical_not(lo_half), xt, jnp.zeros_like(xt))
                part = _dot(mh, xh)
                yd = part if yd is None else yd + part
            y_ref[:, tsl] = yd + y_off[:, p * LANE:(p + 1) * LANE] + xs_a[:, tsl] * dskip[:, tsl]

    yz = y_ref[...] * _silu(z_ref[...].astype(F32))
    ms = jnp.mean(yz * yz, axis=-1, keepdims=True)
    o_ref[...] = (yz * lax.rsqrt(ms + EPS) * nw_ref[...]).astype(BF16)


def _ssd_call(proj, dt_raw, conv_w, conv_b, dt_bias_p, a_log_p, d_skip_x, ssd_norm_w, *, layer, batch, seq,
              d_ssd, n_heads):
    nc = seq // CHUNK
    l = layer
    xbc_w = conv_w.shape[2]
    bc_w = xbc_w - d_ssd
    bc_blk = (6 * d_ssd) // bc_w
    depth = conv_w.shape[0]
    hpg = n_heads // SSD_GROUPS
    return pl.pallas_call(
        functools.partial(_ssd_kernel, n_heads=n_heads, d_ssd=d_ssd),
        grid=(batch, nc),
        in_specs=[
            pl.BlockSpec((CHUNK, d_ssd), lambda b, c: (b * nc + c, 0)),
            pl.BlockSpec((CHUNK, d_ssd), lambda b, c: (b * nc + c, 1)),
            pl.BlockSpec((CHUNK, bc_w), lambda b, c: (b * nc + c, bc_blk)),
            pl.BlockSpec((CHUNK, LANE), lambda b, c: (b * nc + c, 0)),
            pl.BlockSpec((None, SSD_CONV, xbc_w), lambda b, c: (l, 0, 0)),
            pl.BlockSpec((None, 1, xbc_w), lambda b, c: (l, 0, 0)),
            pl.BlockSpec((None, 1, LANE), lambda b, c: (l, 0, 0)),
            pl.BlockSpec((None, 1, LANE), lambda b, c: (l, 0, 0)),
            pl.BlockSpec((None, 1, d_ssd), lambda b, c: (l, 0, 0)),
            pl.BlockSpec((None, 1, d_ssd), lambda b, c: (l, 0, 0)),
        ],
        out_specs=pl.BlockSpec((CHUNK, d_ssd), lambda b, c: (b * nc + c, 0)),
        out_shape=jax.ShapeDtypeStruct((batch * seq, d_ssd), BF16),
        scratch_shapes=[
            pltpu.VMEM((CHUNK + 8, xbc_w), F32),
            pltpu.VMEM((SSD_GROUPS, SSD_STATE, hpg * SSD_HEAD_DIM), F32),
            pltpu.VMEM((CHUNK, d_ssd), F32),
        ],
        compiler_params=pltpu.CompilerParams(
            dimension_semantics=("arbitrary", "arbitrary"), vmem_limit_bytes=VMEM_LIMIT),
        name="ssd",
    )(proj, proj, proj, dt_raw, conv_w, conv_b.reshape(depth, 1, xbc_w), dt_bias_p, a_log_p, d_skip_x,
      ssd_norm_w.reshape(depth, 1, d_ssd))


def _ret_kernel(q_ref, k_ref, v_ref, g_ref, cos_ref, sin_ref, gw_ref, gb_ref, o_ref, state_ref, dec_ref,
                *, n_heads, log_g):
    b = pl.program_id(0)
    c = pl.program_id(1)
    dh = RET_HEAD_DIM
    row = lax.broadcasted_iota(jnp.int32, (CHUNK, CHUNK), 0)
    col = lax.broadcasted_iota(jnp.int32, (CHUNK, CHUNK), 1)

    @pl.when((b == 0) & (c == 0))
    def _():
        diff = (row - col).astype(F32)
        for h in range(n_heads):
            dec_ref[h] = jnp.where(diff >= 0, jnp.exp(jnp.maximum(diff, 0.0) * log_g[h]), 0.0)

    @pl.when(c == 0)
    def _():
        state_ref[...] = jnp.zeros(state_ref.shape, F32)

    cosf = cos_ref[...]
    sinf = sin_ref[...]
    pos = lax.broadcasted_iota(jnp.int32, (CHUNK, 1), 0).astype(F32)
    k_scale = dh ** -0.5
    for h in range(n_heads):
        hs = slice(h * dh, (h + 1) * dh)
        qh = q_ref[:, hs].astype(F32)
        kh = k_ref[:, hs].astype(F32)
        qr = qh * cosf + pltpu.roll(qh, dh // 2, 1) * sinf
        kr = (kh * cosf + pltpu.roll(kh, dh // 2, 1) * sinf) * k_scale
        vh = v_ref[:, hs]
        qb = qr.astype(BF16)
        kb = kr.astype(BF16)
        scores = lax.dot_general(qb, kb, (((1,), (1,)), ((), ())), preferred_element_type=F32)
        inner = _dot((scores * dec_ref[h]).astype(BF16), vh)
        k_dec = jnp.exp((CHUNK - 1.0 - pos) * log_g[h])
        q_dec = jnp.exp((pos + 1.0) * log_g[h])
        prev = state_ref[h]
        cross = _dot((qr * q_dec).astype(BF16), prev.astype(BF16))
        kv = lax.dot_general((kr * k_dec).astype(BF16), vh, (((0,), (0,)), ((), ())),
                             preferred_element_type=F32)
        state_ref[h] = prev * math.exp(CHUNK * log_g[h]) + kv
        o = inner + cross
        mu = jnp.mean(o, axis=-1, keepdims=True)
        oc = o - mu
        var = jnp.mean(oc * oc, axis=-1, keepdims=True)
        on = oc * lax.rsqrt(var + EPS) * gw_ref[:, hs] + gb_ref[:, hs]
        o_ref[:, hs] = (_silu(g_ref[:, hs].astype(F32)) * on).astype(BF16)


def _ret_call(proj, cos_t, sin_t, gn_w, gn_b, *, layer, batch, seq, d_ret, n_heads):
    nc = seq // CHUNK
    l = layer
    depth = gn_w.shape[0]
    log_g = tuple(math.log(1.0 - 2.0 ** (-5.0 - h)) for h in range(n_heads))
    return pl.pallas_call(
        functools.partial(_ret_kernel, n_heads=n_heads, log_g=log_g),
        grid=(batch, nc),
        in_specs=[
            pl.BlockSpec((CHUNK, d_ret), lambda b, c: (b * nc + c, 2)),
            pl.BlockSpec((CHUNK, d_ret), lambda b, c: (b * nc + c, 3)),
            pl.BlockSpec((CHUNK, d_ret), lambda b, c: (b * nc + c, 4)),
            pl.BlockSpec((CHUNK, d_ret), lambda b, c: (b * nc + c, 5)),
            pl.BlockSpec((CHUNK, RET_HEAD_DIM), lambda b, c: (c, 0)),
            pl.BlockSpec((CHUNK, RET_HEAD_DIM), lambda b, c: (c, 0)),
            pl.BlockSpec((None, 1, d_ret), lambda b, c: (l, 0, 0)),
            pl.BlockSpec((None, 1, d_ret), lambda b, c: (l, 0, 0)),
        ],
        out_specs=pl.BlockSpec((CHUNK, d_ret), lambda b, c: (b * nc + c, 0)),
        out_shape=jax.ShapeDtypeStruct((batch * seq, d_ret), BF16),
        scratch_shapes=[
            pltpu.VMEM((n_heads, RET_HEAD_DIM, RET_HEAD_DIM), F32),
            pltpu.VMEM((n_heads, CHUNK, CHUNK), F32),
        ],
        compiler_params=pltpu.CompilerParams(
            dimension_semantics=("arbitrary", "arbitrary"), vmem_limit_bytes=VMEM_LIMIT),
        name="retention",
    )(proj, proj, proj, proj, cos_t, sin_t, gn_w.reshape(depth, 1, d_ret), gn_b.reshape(depth, 1, d_ret))


def _outproj_kernel(x_ref, mod_ref, ys_ref, yr_ref, w_ref, o_ref, wb_ref, *, d_ssd, tn):
    @pl.when(pl.program_id(0) == 0)
    def _():
        def cast(r, carry):
            rows = pl.ds(pl.multiple_of(r * NORM_ROWS, NORM_ROWS), NORM_ROWS)
            wb_ref[rows, :] = w_ref[rows, :].astype(BF16)
            return carry

        lax.fori_loop(0, w_ref.shape[0] // NORM_ROWS, cast, 0)

    ys = ys_ref[...]
    yr = yr_ref[...]
    for n0 in range(0, o_ref.shape[1], tn):
        cols = slice(n0, n0 + tn)
        acc = _dot(ys, wb_ref[:d_ssd, cols]) + _dot(yr, wb_ref[d_ssd:, cols])
        o_ref[:, cols] = x_ref[:, cols] + mod_ref[5:6, cols] * acc


def _outproj_call(x2, mods4, y_ssd, y_ret, w_out, *, layer, tiles_per_batch, tm):
    m, d = x2.shape
    d_ssd = y_ssd.shape[1]
    d_ret = y_ret.shape[1]
    tr = tm // 2
    tpb = tiles_per_batch * 2
    l = layer
    return pl.pallas_call(
        functools.partial(_outproj_kernel, d_ssd=d_ssd, tn=512),
        grid=(m // tr,),
        in_specs=[
            pl.BlockSpec((tr, d), lambda i: (i, 0)),
            pl.BlockSpec((None, None, N_MOD, d), lambda i: (l, i // tpb, 0, 0)),
            pl.BlockSpec((tr, d_ssd), lambda i: (i, 0)),
            pl.BlockSpec((tr, d_ret), lambda i: (i, 0)),
            pl.BlockSpec((None, d_ssd + d_ret, d), lambda i: (l, 0, 0), pipeline_mode=pl.Buffered(1)),
        ],
        out_specs=pl.BlockSpec((tr, d), lambda i: (i, 0)),
        out_shape=jax.ShapeDtypeStruct((m, d), F32),
        scratch_shapes=[pltpu.VMEM((d_ssd + d_ret, d), BF16)],
        compiler_params=pltpu.CompilerParams(
            dimension_semantics=("arbitrary",), vmem_limit_bytes=VMEM_LIMIT),
        name="outproj",
    )(x2, mods4, y_ssd, y_ret, w_out)


def _pad_lanes(v):
    return jnp.pad(v, ((0, 0), (0, LANE - v.shape[1])))[:, None, :]


def kernel(x, c, norm_w, mod_w, mod_b, ffn_w13, ffn_w2, w_in, conv_w, conv_b, dt_bias, a_log, d_skip,
           ssd_norm_w, ret_norm_w, ret_norm_b, w_out, final_norm_w):
    batch, seq, d = x.shape
    depth = norm_w.shape[0]
    n_heads = dt_bias.shape[1]
    d_ssd = ssd_norm_w.shape[1]
    d_ret = ret_norm_w.shape[1]
    ret_heads = d_ret // RET_HEAD_DIM
    xbc_w = conv_w.shape[2]
    m = batch * seq
    tm = min(ROW_TILE, seq)
    tiles_per_batch = seq // tm

    pad_rows = 16
    c_pad = jnp.pad(c, ((0, pad_rows - batch), (0, 0)))
    mods = _mods_call(c_pad, mod_w, mod_b)[:, :batch].reshape(depth, batch, N_MOD, d)

    o_z, o_xbc, o_dt = 0, d_ssd, d_ssd + xbc_w
    o_q = o_dt + n_heads
    w_main = jnp.concatenate(
        [w_in[:, :, o_z:o_z + 2 * d_ssd], w_in[:, :, o_q:o_q + 4 * d_ret], w_in[:, :, o_xbc + d_ssd:o_dt]],
        axis=-1).astype(BF16)
    w_dt = jnp.pad(w_in[:, :, o_dt:o_q], ((0, 0), (0, 0), (0, LANE - n_heads))).astype(BF16)

    half = RET_HEAD_DIM // 2
    inv = ROPE_BASE ** (-jnp.arange(0, RET_HEAD_DIM, 2, dtype=F32) / RET_HEAD_DIM)
    ang = jnp.arange(seq, dtype=F32)[:, None] * inv[None, :]
    cos_t = jnp.concatenate([jnp.cos(ang), jnp.cos(ang)], axis=-1)
    sin_t = jnp.concatenate([-jnp.sin(ang), jnp.sin(ang)], axis=-1)
    assert cos_t.shape == (seq, 2 * half)

    dt_bias_p = _pad_lanes(dt_bias)
    a_log_p = _pad_lanes(a_log)
    d_skip_x = jnp.repeat(d_skip, SSD_HEAD_DIM, axis=1)[:, None, :]
    final_w = final_norm_w.reshape(1, d)

    x2 = x.reshape(m, d)
    common = dict(tiles_per_batch=tiles_per_batch, tm=tm)
    for l in range(depth):
        x2 = _ffn_call(x2, mods, norm_w, final_w, ffn_w13, ffn_w2, layer=l, half=0, sub=0, k0=0,
                       final=False, **common)
        proj, dt_raw = _inproj_call(x2, mods, norm_w, w_main, w_dt, layer=l, **common)
        y_ssd = _ssd_call(proj, dt_raw, conv_w, conv_b, dt_bias_p, a_log_p, d_skip_x, ssd_norm_w, layer=l,
                          batch=batch, seq=seq, d_ssd=d_ssd, n_heads=n_heads)
        y_ret = _ret_call(proj, cos_t, sin_t, ret_norm_w, ret_norm_b, layer=l, batch=batch, seq=seq,
                          d_ret=d_ret, n_heads=ret_heads)
        x2 = _outproj_call(x2, mods, y_ssd, y_ret, w_out, layer=l, **common)
        x2 = _ffn_call(x2, mods, norm_w, final_w, ffn_w13, ffn_w2, layer=l, half=1, sub=2, k0=6,
                       final=(l == depth - 1), **common)
    return x2.reshape(batch, seq, d)
```

```python
import functools
import math

import jax
import jax.numpy as jnp
from jax import lax
from jax.experimental import pallas as pl
from jax.experimental.pallas import tpu as pltpu

F32 = jnp.float32
BF16 = jnp.bfloat16
EPS = 1e-6
LANE = 128
CHUNK = 128
SSD_HEAD_DIM = 64
SSD_GROUPS = 2
SSD_STATE = 128
SSD_CONV = 4
RET_HEAD_DIM = 128
ROPE_BASE = 10000.0
N_MOD = 9
VMEM_LIMIT = 56 * 1024 * 1024
ROW_TILE = 1024
NORM_ROWS = 32


def _silu(x):
    return x * jax.nn.sigmoid(x)


def _split3(x):
    hi = x.astype(BF16)
    r1 = x - hi.astype(F32)
    mid = r1.astype(BF16)
    lo = (r1 - mid.astype(F32)).astype(BF16)
    return hi, mid, lo


def _dot(a, b):
    return jnp.dot(a, b, preferred_element_type=F32)


def _dot_exact_rhs(m_bf16, x):
    hi, mid, lo = _split3(x)
    return _dot(m_bf16, hi) + _dot(m_bf16, mid) + _dot(m_bf16, lo)


def _dot_exact_lhs(x, m_bf16):
    hi, mid, lo = _split3(x)
    return _dot(hi, m_bf16) + _dot(mid, m_bf16) + _dot(lo, m_bf16)


def _dot_hi_mid_lhs(x, m_bf16):
    hi = x.astype(BF16)
    mid = (x - hi.astype(F32)).astype(BF16)
    return _dot(hi, m_bf16) + _dot(mid, m_bf16)


def _norm_modulate_rows(x_ref, h_ref, mult, shift, rows_total, copy_ref=None):
    def body(r, carry):
        rows = pl.ds(pl.multiple_of(r * NORM_ROWS, NORM_ROWS), NORM_ROWS)
        xv = x_ref[rows, :]
        ms = jnp.mean(xv * xv, axis=-1, keepdims=True)
        h_ref[rows, :] = (x_ref[rows, :] * lax.rsqrt(ms + EPS) * mult + shift).astype(BF16)
        if copy_ref is not None:
            copy_ref[rows, :] = xv
        return carry

    lax.fori_loop(0, rows_total // NORM_ROWS, body, 0, unroll=2)


def _mods_kernel(c_ref, w_ref, b_ref, o_ref):
    ca = _silu(c_ref[...]).astype(BF16)
    o_ref[...] = _dot(ca, w_ref[...].astype(BF16)) + b_ref[...]


def _mods_call(c_pad, mod_w, mod_b):
    depth, d, n = mod_w.shape
    rows = c_pad.shape[0]
    tn = 2048
    return pl.pallas_call(
        _mods_kernel,
        grid=(depth, n // tn),
        in_specs=[
            pl.BlockSpec((rows, d), lambda l, j: (0, 0)),
            pl.BlockSpec((None, d, tn), lambda l, j: (l, 0, j)),
            pl.BlockSpec((None, 1, tn), lambda l, j: (l, 0, j)),
        ],
        out_specs=pl.BlockSpec((None, rows, tn), lambda l, j: (l, 0, j)),
        out_shape=jax.ShapeDtypeStruct((depth, rows, n), F32),
        compiler_params=pltpu.CompilerParams(
            dimension_semantics=("arbitrary", "arbitrary"), vmem_limit_bytes=VMEM_LIMIT),
        name="mods",
    )(c_pad, mod_w, mod_b.reshape(depth, 1, n))


def _regroup_kernel(a_ref, b_ref, o_ref, dt_ref, *, first_shifted, n_shifted, lane_off, n_dt):
    j = pl.program_id(1)
    lane = lax.broadcasted_iota(jnp.int32, (a_ref.shape[0], LANE), 1)
    shifted = (j >= first_shifted) & (j < first_shifted + n_shifted)

    @pl.when(j == 0)
    def _():
        dt_ref[...] = jnp.where(lane < n_dt, b_ref[...], 0.0).astype(BF16)

    @pl.when(jnp.logical_not(shifted))
    def _():
        o_ref[...] = a_ref[...].astype(BF16)

    @pl.when(shifted)
    def _():
        nt = a_ref.shape[1] // LANE
        tiles = [a_ref[:, t * LANE:(t + 1) * LANE] for t in range(nt)] + [b_ref[...]]
        rolled = [pltpu.roll(t, LANE - lane_off, 1) for t in tiles]
        for t in range(nt):
            o_ref[:, t * LANE:(t + 1) * LANE] = jnp.where(
                lane < LANE - lane_off, rolled[t], rolled[t + 1]).astype(BF16)


def _regroup_call(w_in, *, d_ssd, xbc_w, n_heads, d_ret):
    depth, d, _ = w_in.shape
    tn = 512
    o_dt = d_ssd + xbc_w
    lane_off = n_heads
    n_front = (2 * d_ssd) // tn
    n_shift = (4 * d_ret) // tn
    n_out = n_front + n_shift + (xbc_w - d_ssd) // tn
    assert o_dt % tn == 0 and (xbc_w - d_ssd) == tn and lane_off < LANE
    dt_blk = o_dt // LANE

    def a_map(l, j):
        blk = jnp.where(j < n_front, j, jnp.where(j < n_front + n_shift, j + 1, n_front))
        return (l, 0, blk)

    def b_map(l, j):
        inside = (j >= n_front) & (j < n_front + n_shift)
        return (l, 0, jnp.where(inside, dt_blk + (tn // LANE) * (j - n_front + 1), dt_blk))

    return pl.pallas_call(
        functools.partial(_regroup_kernel, first_shifted=n_front, n_shifted=n_shift, lane_off=lane_off,
                          n_dt=n_heads),
        grid=(depth, n_out),
        in_specs=[
            pl.BlockSpec((None, d, tn), a_map),
            pl.BlockSpec((None, d, LANE), b_map),
        ],
        out_specs=[
            pl.BlockSpec((None, d, tn), lambda l, j: (l, 0, j)),
            pl.BlockSpec((None, d, LANE), lambda l, j: (l, 0, 0)),
        ],
        out_shape=[
            jax.ShapeDtypeStruct((depth, d, n_out * tn), BF16),
            jax.ShapeDtypeStruct((depth, d, LANE), BF16),
        ],
        compiler_params=pltpu.CompilerParams(
            dimension_semantics=("arbitrary", "arbitrary"), vmem_limit_bytes=VMEM_LIMIT),
        name="regroup_w_in",
    )(w_in, w_in)


def _ffn_kernel(x_ref, mod_ref, nw_ref, fw_ref, wa0_ref, wa1_ref, wb0_ref, wb1_ref, w2a_ref, w2b_ref,
                o_ref, h_ref, wcat_ref, w2c_ref, *, sub, k0, nj, has_tail, final, tm, d):
    j = pl.program_id(1)

    @pl.when(j == 0)
    def _():
        shift = mod_ref[k0:k0 + 1, :]
        scale = mod_ref[k0 + 1:k0 + 2, :]
        mult = nw_ref[sub:sub + 1, :] * (1.0 + scale)
        _norm_modulate_rows(x_ref, h_ref, mult, shift, tm, copy_ref=o_ref)

    wcat_ref[:, 0 * LANE:1 * LANE] = wa0_ref[...].astype(BF16)
    wcat_ref[:, 1 * LANE:2 * LANE] = wa1_ref[...].astype(BF16)
    wcat_ref[:, 2 * LANE:3 * LANE] = wb0_ref[...].astype(BF16)
    wcat_ref[:, 3 * LANE:4 * LANE] = wb1_ref[...].astype(BF16)
    ab = _dot(h_ref[...], wcat_ref[...])
    a = ab[:, :2 * LANE]
    b = ab[:, 2 * LANE:]
    g = _silu(a) * b
    w2_hi = w2b_ref[...]
    if has_tail:
        live = j < nj - 1
        col = lax.broadcasted_iota(jnp.int32, g.shape, 1)
        g = jnp.where((col < LANE) | live, g, 0.0)
        w2_hi = jnp.where(live, w2_hi, 0.0)
    gb = g.astype(BF16)
    w2c_ref[0:LANE, :] = w2a_ref[...].astype(BF16)
    w2c_ref[LANE:2 * LANE, :] = w2_hi.astype(BF16)
    nc = 512
    for n0 in range(0, d, nc):
        half_gate = 0.5 * mod_ref[k0 + 2:k0 + 3, n0:n0 + nc]
        o_ref[:, n0:n0 + nc] += half_gate * _dot(gb, w2c_ref[:, n0:n0 + nc])

    if final:
        @pl.when(j == nj - 1)
        def _():
            fw = fw_ref[...]

            def fin(r, carry):
                rows = pl.ds(pl.multiple_of(r * NORM_ROWS, NORM_ROWS), NORM_ROWS)
                xn = o_ref[rows, :]
                ms = jnp.mean(xn * xn, axis=-1, keepdims=True)
                o_ref[rows, :] = xn * lax.rsqrt(ms + EPS) * fw
                return carry

            lax.fori_loop(0, tm // NORM_ROWS, fin, 0, unroll=2)


def _ffn_call(x2, mods4, norm_w, final_w, w13, w2, *, layer, half, sub, k0, final, tiles_per_batch, tm):
    m, d = x2.shape
    d_ff = w2.shape[2]
    nblk = d_ff // LANE
    nj = (nblk + 1) // 2
    has_tail = nblk % 2 == 1
    last = nblk - 1
    l, s = layer, half
    kern = functools.partial(_ffn_kernel, sub=sub, k0=k0, nj=nj, has_tail=has_tail, final=final, tm=tm, d=d)
    return pl.pallas_call(
        kern,
        grid=(m // tm, nj),
        in_specs=[
            pl.BlockSpec((tm, d), lambda i, j: (i, 0)),
            pl.BlockSpec((None, None, N_MOD, d), lambda i, j: (l, i // tiles_per_batch, 0, 0)),
            pl.BlockSpec((None, 3, d), lambda i, j: (l, 0, 0)),
            pl.BlockSpec((1, d), lambda i, j: (0, 0)),
            pl.BlockSpec((None, None, d, LANE), lambda i, j: (l, s, 0, 2 * j)),
            pl.BlockSpec((None, None, d, LANE), lambda i, j: (l, s, 0, jnp.minimum(2 * j + 1, last))),
            pl.BlockSpec((None, None, d, LANE), lambda i, j: (l, s, 0, nblk + 2 * j)),
            pl.BlockSpec((None, None, d, LANE), lambda i, j: (l, s, 0, nblk + jnp.minimum(2 * j + 1, last))),
            pl.BlockSpec((None, None, LANE, d), lambda i, j: (l, s, 2 * j, 0)),
            pl.BlockSpec((None, None, LANE, d), lambda i, j: (l, s, jnp.minimum(2 * j + 1, last), 0)),
        ],
        out_specs=pl.BlockSpec((tm, d), lambda i, j: (i, 0)),
        out_shape=jax.ShapeDtypeStruct((m, d), F32),
        scratch_shapes=[
            pltpu.VMEM((tm, d), BF16),
            pltpu.VMEM((d, 4 * LANE), BF16),
            pltpu.VMEM((2 * LANE, d), BF16),
        ],
        compiler_params=pltpu.CompilerParams(
            dimension_semantics=("arbitrary", "arbitrary"), vmem_limit_bytes=VMEM_LIMIT),
        name=f"ffn_{sub}",
    )(x2, mods4, norm_w, final_w, w13, w13, w13, w13, w2, w2)


def _inproj_kernel(x_ref, mod_ref, nw_ref, w_ref, wdt_ref, o_ref, dt_ref, h_ref, *, tm, tn):
    shift = mod_ref[3:4, :]
    scale = mod_ref[4:5, :]
    mult = nw_ref[1:2, :] * (1.0 + scale)
    _norm_modulate_rows(x_ref, h_ref, mult, shift, tm)
    h = h_ref[...]
    dt_ref[...] = _dot(h, wdt_ref[...])
    for n0 in range(0, o_ref.shape[1], tn):
        o_ref[:, n0:n0 + tn] = _dot(h, w_ref[:, n0:n0 + tn]).astype(BF16)


def _inproj_call(x2, mods4, norm_w, w_main, w_dt, *, layer, tiles_per_batch, tm):
    m, d = x2.shape
    n = w_main.shape[2]
    tr = tm // 2
    tpb = tiles_per_batch * 2
    l = layer
    once = pl.Buffered(1)
    return pl.pallas_call(
        functools.partial(_inproj_kernel, tm=tr, tn=512),
        grid=(m // tr,),
        in_specs=[
            pl.BlockSpec((tr, d), lambda i: (i, 0)),
            pl.BlockSpec((None, None, N_MOD, d), lambda i: (l, i // tpb, 0, 0)),
            pl.BlockSpec((None, 3, d), lambda i: (l, 0, 0)),
            pl.BlockSpec((None, d, n), lambda i: (l, 0, 0), pipeline_mode=once),
            pl.BlockSpec((None, d, LANE), lambda i: (l, 0, 0), pipeline_mode=once),
        ],
        out_specs=[
            pl.BlockSpec((tr, n), lambda i: (i, 0)),
            pl.BlockSpec((tr, LANE), lambda i: (i, 0)),
        ],
        out_shape=[
            jax.ShapeDtypeStruct((m, n), BF16),
            jax.ShapeDtypeStruct((m, LANE), F32),
        ],
        scratch_shapes=[pltpu.VMEM((tr, d), BF16)],
        compiler_params=pltpu.CompilerParams(
            dimension_semantics=("arbitrary",), vmem_limit_bytes=VMEM_LIMIT),
        name="inproj",
    )(x2, mods4, norm_w, w_main, w_dt)


def _ssd_body(z_ref, xs_ref, bc_ref, pxs_ref, pbc_ref, dt_ref, cw_ref, cb_ref, dtb_ref, alog_ref, dskip_ref, nw_ref,
              o_ref, state_ref, y_ref, *, n_heads, d_ssd, first_chunk):
    gw = SSD_GROUPS * SSD_STATE
    hpg = n_heads // SSD_GROUPS
    gch = hpg * SSD_HEAD_DIM

    srow = lax.broadcasted_iota(jnp.int32, ((SSD_CONV - 1) * CHUNK, 2 * CHUNK), 0)
    scol = lax.broadcasted_iota(jnp.int32, ((SSD_CONV - 1) * CHUNK, 2 * CHUNK), 1)
    src = (srow & (CHUNK - 1)) - (srow // CHUNK + 1) + CHUNK
    keep = (scol >= CHUNK) | jnp.logical_not(first_chunk)
    shift = jnp.where((scol == src) & keep, 1.0, 0.0).astype(BF16)

    def conv_part(cur_ref, prev_ref, lo, hi):
        cur = cur_ref[...]
        sh = _dot(shift, jnp.concatenate([prev_ref[...], cur], axis=0))
        acc = cb_ref[:, lo:hi] + cw_ref[SSD_CONV - 1:SSD_CONV, lo:hi] * cur.astype(F32)
        for dd in range(1, SSD_CONV):
            k = SSD_CONV - 1 - dd
            acc = acc + cw_ref[k:k + 1, lo:hi] * sh[(dd - 1) * CHUNK:dd * CHUNK, :]
        return acc

    conv_xs = conv_part(xs_ref, pxs_ref, 0, d_ssd)
    conv_bc = conv_part(bc_ref, pbc_ref, d_ssd, d_ssd + 2 * gw)
    xs_a = _silu(conv_xs)
    bc_a = _silu(conv_bc)
    bm = bc_a[:, :gw]
    cm = bc_a[:, gw:]

    lane = lax.broadcasted_iota(jnp.int32, (CHUNK, LANE), 1)
    row = lax.broadcasted_iota(jnp.int32, (CHUNK, LANE), 0)
    head_lane = lane < n_heads
    dt_in = dt_ref[...] + dtb_ref[...]
    dt = jnp.maximum(dt_in, 0.0) + jnp.log1p(jnp.exp(-jnp.abs(dt_in)))
    dt = jnp.where(head_lane, dt, 0.0)
    a = dt * (-jnp.exp(alog_ref[...]))
    causal = row >= lane
    tri = jnp.where(causal, 1.0, 0.0).astype(BF16)
    a_cum = _dot_exact_rhs(tri, a)
    a_cum_t = a_cum.T
    ea = jnp.exp(a_cum)
    dec = jnp.exp(a_cum[CHUNK - 1:CHUNK, :] - a_cum)

    erow = lax.broadcasted_iota(jnp.int32, (LANE, d_ssd), 0)
    ecol = lax.broadcasted_iota(jnp.int32, (LANE, d_ssd), 1)
    expand = jnp.where(ecol // SSD_HEAD_DIM == erow, 1.0, 0.0).astype(BF16)
    dt_x = _dot_hi_mid_lhs(dt, expand)
    dtdec_x = _dot_hi_mid_lhs(dt * dec, expand)
    ea_x = _dot_exact_lhs(ea, expand)
    xdt_b = (xs_a * dt_x).astype(BF16)
    xsd_b = (xs_a * dtdec_x).astype(BF16)
    dskip = dskip_ref[...]

    lo_half = lane < SSD_HEAD_DIM
    pairs = hpg // 2
    for g in range(SSD_GROUPS):
        bg = bm[:, g * SSD_STATE:(g + 1) * SSD_STATE].astype(BF16)
        cg = cm[:, g * SSD_STATE:(g + 1) * SSD_STATE].astype(BF16)
        cb = lax.dot_general(cg, bg, (((1,), (1,)), ((), ())), preferred_element_type=F32)
        gsl = slice(g * gch, (g + 1) * gch)
        prev_t = state_ref[g]
        y_off = _dot(cg, prev_t.astype(BF16)) * ea_x[:, gsl]
        states_t = lax.dot_general(bg, xsd_b[:, gsl], (((0,), (0,)), ((), ())),
                                   preferred_element_type=F32)
        state_ref[g] = prev_t * ea_x[CHUNK - 1:CHUNK, gsl] + states_t
        for p in range(pairs):
            t = g * pairs + p
            tsl = slice(t * LANE, (t + 1) * LANE)
            xt = xdt_b[:, tsl]
            yd = None
            for q in range(2):
                h = 2 * t + q
                seg = a_cum[:, h:h + 1] - a_cum_t[h:h + 1, :]
                lmat = jnp.exp(jnp.where(causal, seg, -jnp.inf))
                mh = (cb * lmat).astype(BF16)
                xh = jnp.where(lo_half if q == 0 else jnp.logical_not(lo_half), xt, jnp.zeros_like(xt))
                part = _dot(mh, xh)
                yd = part if yd is None else yd + part
            y_ref[:, tsl] = yd + y_off[:, p * LANE:(p + 1) * LANE] + xs_a[:, tsl] * dskip[:, tsl]

    yz = y_ref[...] * _silu(z_ref[...].astype(F32))
    ms = jnp.mean(yz * yz, axis=-1, keepdims=True)
    o_ref[:, :d_ssd] = (yz * lax.rsqrt(ms + EPS) * nw_ref[...]).astype(BF16)


def _ret_body(q_ref, k_ref, v_ref, g_ref, cos_ref, sin_ref, gw_ref, gb_ref, o_ref, state_ref, qd_ref, kd_ref,
              *, n_heads, log_g, col0):
    dh = RET_HEAD_DIM
    cosf = cos_ref[...]
    sinf = sin_ref[...]
    row = lax.broadcasted_iota(jnp.int32, (CHUNK, CHUNK), 0)
    col = lax.broadcasted_iota(jnp.int32, (CHUNK, CHUNK), 1)
    causal = row >= col
    for h in range(n_heads):
        hs = slice(h * dh, (h + 1) * dh)
        qh = q_ref[:, hs].astype(F32)
        kh = k_ref[:, hs].astype(F32)
        qt = ((qh * cosf + pltpu.roll(qh, dh // 2, 1) * sinf) * qd_ref[h]).astype(BF16)
        kt = ((kh * cosf + pltpu.roll(kh, dh // 2, 1) * sinf) * kd_ref[h]).astype(BF16)
        vh = v_ref[:, hs]
        scores = lax.dot_general(qt, kt, (((1,), (1,)), ((), ())), preferred_element_type=F32)
        inner = _dot(jnp.where(causal, scores, 0.0).astype(BF16), vh)
        prev = state_ref[h]
        cross = _dot(qt, prev.astype(BF16))
        kv = lax.dot_general(kt, vh, (((0,), (0,)), ((), ())), preferred_element_type=F32)
        state_ref[h] = (prev + kv) * math.exp(CHUNK * log_g[h])
        o = inner + cross
        mu = jnp.mean(o, axis=-1, keepdims=True)
        oc = o - mu
        var = jnp.mean(oc * oc, axis=-1, keepdims=True)
        on = oc * lax.rsqrt(var + EPS) * gw_ref[:, hs] + gb_ref[:, hs]
        o_ref[:, col0 + h * dh:col0 + (h + 1) * dh] = (_silu(g_ref[:, hs].astype(F32)) * on).astype(BF16)


def _mixer_kernel(z_ref, xs_ref, bc_ref, pxs_ref, pbc_ref, dt_ref, q_ref, k_ref, v_ref, g_ref, cos_ref, sin_ref,
                  cw_ref, cb_ref, dtb_ref, alog_ref, dskip_ref, nw_ref, gw_ref, gb_ref,
                  o_ref, sstate_ref, y_ref, rstate_ref, qd_ref, kd_ref,
                  *, n_heads, d_ssd, ret_heads, log_g):
    b = pl.program_id(0)
    c = pl.program_id(1)

    @pl.when((b == 0) & (c == 0))
    def _():
        pos = lax.broadcasted_iota(jnp.int32, (CHUNK, RET_HEAD_DIM), 0).astype(F32) + 1.0
        for h in range(ret_heads):
            qd_ref[h] = jnp.exp(pos * log_g[h])
            kd_ref[h] = jnp.exp(pos * (-log_g[h])) * (RET_HEAD_DIM ** -0.5)

    @pl.when(c == 0)
    def _():
        sstate_ref[...] = jnp.zeros(sstate_ref.shape, F32)
        rstate_ref[...] = jnp.zeros(rstate_ref.shape, F32)

    _ssd_body(z_ref, xs_ref, bc_ref, pxs_ref, pbc_ref, dt_ref, cw_ref, cb_ref, dtb_ref, alog_ref, dskip_ref, nw_ref,
              o_ref, sstate_ref, y_ref, n_heads=n_heads, d_ssd=d_ssd, first_chunk=c == 0)
    _ret_body(q_ref, k_ref, v_ref, g_ref, cos_ref, sin_ref, gw_ref, gb_ref, o_ref, rstate_ref, qd_ref, kd_ref,
              n_heads=ret_heads, log_g=log_g, col0=d_ssd)


def _mixer_call(proj, dt_raw, cos_t, sin_t, conv_w, conv_b, dt_bias_p, a_log_p, d_skip_x, ssd_norm_w, gn_w, gn_b,
                *, layer, batch, seq, d_ssd, n_heads, d_ret, ret_heads):
    nc = seq // CHUNK
    l = layer
    depth = conv_w.shape[0]
    xbc_w = conv_w.shape[2]
    bc_w = xbc_w - d_ssd
    bc_blk = (2 * d_ssd + 4 * d_ret) // bc_w
    hpg = n_heads // SSD_GROUPS
    assert d_ssd == d_ret
    log_g = tuple(math.log(1.0 - 2.0 ** (-5.0 - h)) for h in range(ret_heads))
    tok = lambda blk: (lambda b, c: (b * nc + c, blk))
    before = lambda blk: (lambda b, c: (b * nc + jnp.maximum(c - 1, 0), blk))
    par = lambda b, c: (l, 0, 0)
    return pl.pallas_call(
        functools.partial(_mixer_kernel, n_heads=n_heads, d_ssd=d_ssd, ret_heads=ret_heads, log_g=log_g),
        grid=(batch, nc),
        in_specs=[
            pl.BlockSpec((CHUNK, d_ssd), tok(0)),
            pl.BlockSpec((CHUNK, d_ssd), tok(1)),
            pl.BlockSpec((CHUNK, bc_w), tok(bc_blk)),
            pl.BlockSpec((CHUNK, d_ssd), before(1)),
            pl.BlockSpec((CHUNK, bc_w), before(bc_blk)),
            pl.BlockSpec((CHUNK, LANE), tok(0)),
            pl.BlockSpec((CHUNK, d_ret), tok(2)),
            pl.BlockSpec((CHUNK, d_ret), tok(3)),
            pl.BlockSpec((CHUNK, d_ret), tok(4)),
            pl.BlockSpec((CHUNK, d_ret), tok(5)),
            pl.BlockSpec((CHUNK, RET_HEAD_DIM), lambda b, c: (c, 0)),
            pl.BlockSpec((CHUNK, RET_HEAD_DIM), lambda b, c: (c, 0)),
            pl.BlockSpec((None, SSD_CONV, xbc_w), par),
            pl.BlockSpec((None, 1, xbc_w), par),
            pl.BlockSpec((None, 1, LANE), par),
            pl.BlockSpec((None, 1, LANE), par),
            pl.BlockSpec((None, 1, d_ssd), par),
            pl.BlockSpec((None, 1, d_ssd), par),
            pl.BlockSpec((None, 1, d_ret), par),
            pl.BlockSpec((None, 1, d_ret), par),
        ],
        out_specs=pl.BlockSpec((CHUNK, d_ssd + d_ret), lambda b, c: (b * nc + c, 0)),
        out_shape=jax.ShapeDtypeStruct((batch * seq, d_ssd + d_ret), BF16),
        scratch_shapes=[
            pltpu.VMEM((SSD_GROUPS, SSD_STATE, hpg * SSD_HEAD_DIM), F32),
            pltpu.VMEM((CHUNK, d_ssd), F32),
            pltpu.VMEM((ret_heads, RET_HEAD_DIM, RET_HEAD_DIM), F32),
            pltpu.VMEM((ret_heads, CHUNK, RET_HEAD_DIM), F32),
            pltpu.VMEM((ret_heads, CHUNK, RET_HEAD_DIM), F32),
        ],
        compiler_params=pltpu.CompilerParams(
            dimension_semantics=("arbitrary", "arbitrary"), vmem_limit_bytes=VMEM_LIMIT),
        name="mixer",
    )(proj, proj, proj, proj, proj, dt_raw, proj, proj, proj, proj, cos_t, sin_t,
      conv_w, conv_b.reshape(depth, 1, xbc_w), dt_bias_p, a_log_p, d_skip_x,
      ssd_norm_w.reshape(depth, 1, d_ssd), gn_w.reshape(depth, 1, d_ret), gn_b.reshape(depth, 1, d_ret))


def _outproj_kernel(x_ref, mod_ref, y_ref, w_ref, o_ref, wb_ref, *, tn):
    @pl.when(pl.program_id(0) == 0)
    def _():
        def cast(r, carry):
            rows = pl.ds(pl.multiple_of(r * NORM_ROWS, NORM_ROWS), NORM_ROWS)
            wb_ref[rows, :] = w_ref[rows, :].astype(BF16)
            return carry

        lax.fori_loop(0, w_ref.shape[0] // NORM_ROWS, cast, 0)

    y = y_ref[...]
    for n0 in range(0, o_ref.shape[1], tn):
        cols = slice(n0, n0 + tn)
        o_ref[:, cols] = x_ref[:, cols] + mod_ref[5:6, cols] * _dot(y, wb_ref[:, cols])


def _outproj_call(x2, mods4, y_mix, w_out, *, layer, tiles_per_batch, tm):
    m, d = x2.shape
    d_mix = y_mix.shape[1]
    tr = tm // 2
    tpb = tiles_per_batch * 2
    l = layer
    return pl.pallas_call(
        functools.partial(_outproj_kernel, tn=512),
        grid=(m // tr,),
        in_specs=[
            pl.BlockSpec((tr, d), lambda i: (i, 0)),
            pl.BlockSpec((None, None, N_MOD, d), lambda i: (l, i // tpb, 0, 0)),
            pl.BlockSpec((tr, d_mix), lambda i: (i, 0)),
            pl.BlockSpec((None, d_mix, d), lambda i: (l, 0, 0), pipeline_mode=pl.Buffered(1)),
        ],
        out_specs=pl.BlockSpec((tr, d), lambda i: (i, 0)),
        out_shape=jax.ShapeDtypeStruct((m, d), F32),
        scratch_shapes=[pltpu.VMEM((d_mix, d), BF16)],
        compiler_params=pltpu.CompilerParams(
            dimension_semantics=("arbitrary",), vmem_limit_bytes=VMEM_LIMIT),
        name="outproj",
    )(x2, mods4, y_mix, w_out)


def _pad_lanes(v):
    return jnp.pad(v, ((0, 0), (0, LANE - v.shape[1])))[:, None, :]


def kernel(x, c, norm_w, mod_w, mod_b, ffn_w13, ffn_w2, w_in, conv_w, conv_b, dt_bias, a_log, d_skip,
           ssd_norm_w, ret_norm_w, ret_norm_b, w_out, final_norm_w):
    batch, seq, d = x.shape
    depth = norm_w.shape[0]
    n_heads = dt_bias.shape[1]
    d_ssd = ssd_norm_w.shape[1]
    d_ret = ret_norm_w.shape[1]
    ret_heads = d_ret // RET_HEAD_DIM
    xbc_w = conv_w.shape[2]
    m = batch * seq
    tm = min(ROW_TILE, seq)
    tiles_per_batch = seq // tm

    pad_rows = 16
    c_pad = jnp.pad(c, ((0, pad_rows - batch), (0, 0)))
    mods = _mods_call(c_pad, mod_w, mod_b)[:, :batch].reshape(depth, batch, N_MOD, d)

    w_main, w_dt = _regroup_call(w_in, d_ssd=d_ssd, xbc_w=xbc_w, n_heads=n_heads, d_ret=d_ret)

    half = RET_HEAD_DIM // 2
    inv = ROPE_BASE ** (-jnp.arange(0, RET_HEAD_DIM, 2, dtype=F32) / RET_HEAD_DIM)
    ang = jnp.arange(seq, dtype=F32)[:, None] * inv[None, :]
    cos_t = jnp.concatenate([jnp.cos(ang), jnp.cos(ang)], axis=-1)
    sin_t = jnp.concatenate([-jnp.sin(ang), jnp.sin(ang)], axis=-1)
    assert cos_t.shape == (seq, 2 * half)

    dt_bias_p = _pad_lanes(dt_bias)
    a_log_p = _pad_lanes(a_log)
    d_skip_x = jnp.repeat(d_skip, SSD_HEAD_DIM, axis=1)[:, None, :]
    final_w = final_norm_w.reshape(1, d)

    x2 = x.reshape(m, d)
    common = dict(tiles_per_batch=tiles_per_batch, tm=tm)
    for l in range(depth):
        x2 = _ffn_call(x2, mods, norm_w, final_w, ffn_w13, ffn_w2, layer=l, half=0, sub=0, k0=0,
                       final=False, **common)
        proj, dt_raw = _inproj_call(x2, mods, norm_w, w_main, w_dt, layer=l, **common)
        y_mix = _mixer_call(proj, dt_raw, cos_t, sin_t, conv_w, conv_b, dt_bias_p, a_log_p, d_skip_x, ssd_norm_w,
                            ret_norm_w, ret_norm_b, layer=l, batch=batch, seq=seq, d_ssd=d_ssd,
                            n_heads=n_heads, d_ret=d_ret, ret_heads=ret_heads)
        x2 = _outproj_call(x2, mods, y_mix, w_out, layer=l, **common)
        x2 = _ffn_call(x2, mods, norm_w, final_w, ffn_w13, ffn_w2, layer=l, half=1, sub=2, k0=6,
                       final=(l == depth - 1), **common)
    return x2.reshape(batch, seq, d)
```

```python
import functools
import math

import jax
import jax.numpy as jnp
from jax import lax
from jax.experimental import pallas as pl
from jax.experimental.pallas import tpu as pltpu

F32 = jnp.float32
BF16 = jnp.bfloat16
EPS = 1e-6
LANE = 128
CHUNK = 128
SSD_HEAD_DIM = 64
SSD_GROUPS = 2
SSD_STATE = 128
SSD_CONV = 4
RET_HEAD_DIM = 128
ROPE_BASE = 10000.0
N_MOD = 9
VMEM_LIMIT = 56 * 1024 * 1024
ROW_TILE = 1024
NORM_ROWS = 32


def _silu(x):
    return x * jax.nn.sigmoid(x)


def _split3(x):
    hi = x.astype(BF16)
    r1 = x - hi.astype(F32)
    mid = r1.astype(BF16)
    lo = (r1 - mid.astype(F32)).astype(BF16)
    return hi, mid, lo


def _dot(a, b):
    return jnp.dot(a, b, preferred_element_type=F32)


def _dot_exact_rhs(m_bf16, x):
    hi, mid, lo = _split3(x)
    return _dot(m_bf16, hi) + _dot(m_bf16, mid) + _dot(m_bf16, lo)


def _dot_exact_lhs(x, m_bf16):
    hi, mid, lo = _split3(x)
    return _dot(hi, m_bf16) + _dot(mid, m_bf16) + _dot(lo, m_bf16)


def _dot_hi_mid_lhs(x, m_bf16):
    hi = x.astype(BF16)
    mid = (x - hi.astype(F32)).astype(BF16)
    return _dot(hi, m_bf16) + _dot(mid, m_bf16)


def _norm_modulate_rows(x_ref, h_ref, mult, shift, rows_total, copy_ref=None):
    def body(r, carry):
        rows = pl.ds(pl.multiple_of(r * NORM_ROWS, NORM_ROWS), NORM_ROWS)
        xv = x_ref[rows, :]
        ms = jnp.mean(xv * xv, axis=-1, keepdims=True)
        h_ref[rows, :] = (x_ref[rows, :] * lax.rsqrt(ms + EPS) * mult + shift).astype(BF16)
        if copy_ref is not None:
            copy_ref[rows, :] = xv
        return carry

    lax.fori_loop(0, rows_total // NORM_ROWS, body, 0, unroll=2)


def _mods_kernel(c_ref, w_ref, b_ref, o_ref):
    ca = _silu(c_ref[...]).astype(BF16)
    o_ref[...] = _dot(ca, w_ref[...].astype(BF16)) + b_ref[...]


def _mods_call(c_pad, mod_w, mod_b):
    depth, d, n = mod_w.shape
    rows = c_pad.shape[0]
    tn = 2048
    return pl.pallas_call(
        _mods_kernel,
        grid=(depth, n // tn),
        in_specs=[
            pl.BlockSpec((rows, d), lambda l, j: (0, 0)),
            pl.BlockSpec((None, d, tn), lambda l, j: (l, 0, j)),
            pl.BlockSpec((None, 1, tn), lambda l, j: (l, 0, j)),
        ],
        out_specs=pl.BlockSpec((None, rows, tn), lambda l, j: (l, 0, j)),
        out_shape=jax.ShapeDtypeStruct((depth, rows, n), F32),
        compiler_params=pltpu.CompilerParams(
            dimension_semantics=("arbitrary", "arbitrary"), vmem_limit_bytes=VMEM_LIMIT),
        name="mods",
    )(c_pad, mod_w, mod_b.reshape(depth, 1, n))


def _regroup_kernel(a_ref, b_ref, o_ref, dt_ref, *, first_shifted, n_shifted, lane_off, n_dt):
    j = pl.program_id(1)
    lane = lax.broadcasted_iota(jnp.int32, (a_ref.shape[0], LANE), 1)
    shifted = (j >= first_shifted) & (j < first_shifted + n_shifted)

    @pl.when(j == 0)
    def _():
        dt_ref[...] = jnp.where(lane < n_dt, b_ref[...], 0.0).astype(BF16)

    @pl.when(jnp.logical_not(shifted))
    def _():
        o_ref[...] = a_ref[...].astype(BF16)

    @pl.when(shifted)
    def _():
        nt = a_ref.shape[1] // LANE
        tiles = [a_ref[:, t * LANE:(t + 1) * LANE] for t in range(nt)] + [b_ref[...]]
        rolled = [pltpu.roll(t, LANE - lane_off, 1) for t in tiles]
        for t in range(nt):
            o_ref[:, t * LANE:(t + 1) * LANE] = jnp.where(
                lane < LANE - lane_off, rolled[t], rolled[t + 1]).astype(BF16)


def _regroup_call(w_in, *, d_ssd, xbc_w, n_heads, d_ret):
    depth, d, _ = w_in.shape
    tn = 512
    o_dt = d_ssd + xbc_w
    lane_off = n_heads
    n_front = (2 * d_ssd) // tn
    n_shift = (4 * d_ret) // tn
    n_out = n_front + n_shift + (xbc_w - d_ssd) // tn
    assert o_dt % tn == 0 and (xbc_w - d_ssd) == tn and lane_off < LANE
    dt_blk = o_dt // LANE

    def a_map(l, j):
        blk = jnp.where(j < n_front, j, jnp.where(j < n_front + n_shift, j + 1, n_front))
        return (l, blk)

    def b_map(l, j):
        inside = (j >= n_front) & (j < n_front + n_shift)
        return (l, jnp.where(inside, dt_blk + (tn // LANE) * (j - n_front + 1), dt_blk))

    return pl.pallas_call(
        functools.partial(_regroup_kernel, first_shifted=n_front, n_shifted=n_shift, lane_off=lane_off,
                          n_dt=n_heads),
        grid=(depth, n_out),
        in_specs=[
            pl.BlockSpec((d, tn), a_map),
            pl.BlockSpec((d, LANE), b_map),
        ],
        out_specs=[
            pl.BlockSpec((None, d, tn), lambda l, j: (l, 0, j)),
            pl.BlockSpec((None, d, LANE), lambda l, j: (l, 0, 0)),
        ],
        out_shape=[
            jax.ShapeDtypeStruct((depth, d, n_out * tn), BF16),
            jax.ShapeDtypeStruct((depth, d, LANE), BF16),
        ],
        compiler_params=pltpu.CompilerParams(
            dimension_semantics=("arbitrary", "arbitrary"), vmem_limit_bytes=VMEM_LIMIT),
        name="regroup_w_in",
    )(w_in.reshape(depth * d, -1), w_in.reshape(depth * d, -1))


def _ffn_kernel(x_ref, mod_ref, nw_ref, fw_ref, wa_ref, wb0_ref, wb1_ref, w2_ref,
                o_ref, h_ref, wcat_ref, w2c_ref, *, sub, k0, nj, has_tail, final, tm, d):
    j = pl.program_id(1)

    @pl.when(j == 0)
    def _():
        shift = mod_ref[k0:k0 + 1, :]
        scale = mod_ref[k0 + 1:k0 + 2, :]
        mult = nw_ref[sub:sub + 1, :] * (1.0 + scale)
        _norm_modulate_rows(x_ref, h_ref, mult, shift, tm, copy_ref=o_ref)

    wcat_ref[:, 0 * LANE:2 * LANE] = wa_ref[...].astype(BF16)
    wcat_ref[:, 2 * LANE:3 * LANE] = wb0_ref[...].astype(BF16)
    wcat_ref[:, 3 * LANE:4 * LANE] = wb1_ref[...].astype(BF16)
    ab = _dot(h_ref[...], wcat_ref[...])
    a = ab[:, :2 * LANE]
    b = ab[:, 2 * LANE:]
    g = _silu(a) * b
    w2_hi = w2_ref[LANE:2 * LANE, :]
    if has_tail:
        live = j < nj - 1
        col = lax.broadcasted_iota(jnp.int32, g.shape, 1)
        g = jnp.where((col < LANE) | live, g, 0.0)
        w2_hi = jnp.where(live, w2_hi, 0.0)
    gb = g.astype(BF16)
    w2c_ref[0:LANE, :] = w2_ref[0:LANE, :].astype(BF16)
    w2c_ref[LANE:2 * LANE, :] = w2_hi.astype(BF16)
    nc = 512
    for n0 in range(0, d, nc):
        half_gate = 0.5 * mod_ref[k0 + 2:k0 + 3, n0:n0 + nc]
        o_ref[:, n0:n0 + nc] += half_gate * _dot(gb, w2c_ref[:, n0:n0 + nc])

    if final:
        @pl.when(j == nj - 1)
        def _():
            fw = fw_ref[...]

            def fin(r, carry):
                rows = pl.ds(pl.multiple_of(r * NORM_ROWS, NORM_ROWS), NORM_ROWS)
                xn = o_ref[rows, :]
                ms = jnp.mean(xn * xn, axis=-1, keepdims=True)
                o_ref[rows, :] = xn * lax.rsqrt(ms + EPS) * fw
                return carry

            lax.fori_loop(0, tm // NORM_ROWS, fin, 0, unroll=2)


def _ffn_call(x2, mods4, norm_w, final_w, w13, w2, *, layer, half, sub, k0, final, tiles_per_batch, tm):
    m, d = x2.shape
    d_ff = w2.shape[2]
    nblk = d_ff // LANE
    nj = (nblk + 1) // 2
    has_tail = nblk % 2 == 1
    last = nblk - 1
    l, s = layer, half
    kern = functools.partial(_ffn_kernel, sub=sub, k0=k0, nj=nj, has_tail=has_tail, final=final, tm=tm, d=d)
    return pl.pallas_call(
        kern,
        grid=(m // tm, nj),
        in_specs=[
            pl.BlockSpec((tm, d), lambda i, j: (i, 0)),
            pl.BlockSpec((None, None, N_MOD, d), lambda i, j: (l, i // tiles_per_batch, 0, 0)),
            pl.BlockSpec((None, 3, d), lambda i, j: (l, 0, 0)),
            pl.BlockSpec((1, d), lambda i, j: (0, 0)),
            pl.BlockSpec((None, None, d, 2 * LANE), lambda i, j: (l, s, 0, j)),
            pl.BlockSpec((None, None, d, LANE), lambda i, j: (l, s, 0, nblk + 2 * j)),
            pl.BlockSpec((None, None, d, LANE), lambda i, j: (l, s, 0, nblk + jnp.minimum(2 * j + 1, last))),
            pl.BlockSpec((None, None, 2 * LANE, d), lambda i, j: (l, s, j, 0)),
        ],
        out_specs=pl.BlockSpec((tm, d), lambda i, j: (i, 0)),
        out_shape=jax.ShapeDtypeStruct((m, d), F32),
        scratch_shapes=[
            pltpu.VMEM((tm, d), BF16),
            pltpu.VMEM((d, 4 * LANE), BF16),
            pltpu.VMEM((2 * LANE, d), BF16),
        ],
        compiler_params=pltpu.CompilerParams(
            dimension_semantics=("arbitrary", "arbitrary"), vmem_limit_bytes=VMEM_LIMIT),
        name=f"ffn_{sub}",
    )(x2, mods4, norm_w, final_w, w13, w13, w13, w2)


def _inproj_kernel(x_ref, mod_ref, nw_ref, w_ref, wdt_ref, o_ref, dt_ref, h_ref, *, tm, tn):
    shift = mod_ref[3:4, :]
    scale = mod_ref[4:5, :]
    mult = nw_ref[1:2, :] * (1.0 + scale)
    _norm_modulate_rows(x_ref, h_ref, mult, shift, tm)
    h = h_ref[...]
    dt_ref[...] = _dot(h, wdt_ref[...])
    for n0 in range(0, o_ref.shape[1], tn):
        o_ref[:, n0:n0 + tn] = _dot(h, w_ref[:, n0:n0 + tn]).astype(BF16)


def _inproj_call(x2, mods4, norm_w, w_main, w_dt, *, layer, tiles_per_batch, tm):
    m, d = x2.shape
    n = w_main.shape[2]
    tr = tm // 2
    tpb = tiles_per_batch * 2
    l = layer
    once = pl.Buffered(1)
    return pl.pallas_call(
        functools.partial(_inproj_kernel, tm=tr, tn=512),
        grid=(m // tr,),
        in_specs=[
            pl.BlockSpec((tr, d), lambda i: (i, 0)),
            pl.BlockSpec((None, None, N_MOD, d), lambda i: (l, i // tpb, 0, 0)),
            pl.BlockSpec((None, 3, d), lambda i: (l, 0, 0)),
            pl.BlockSpec((None, d, n), lambda i: (l, 0, 0), pipeline_mode=once),
            pl.BlockSpec((None, d, LANE), lambda i: (l, 0, 0), pipeline_mode=once),
        ],
        out_specs=[
            pl.BlockSpec((tr, n), lambda i: (i, 0)),
            pl.BlockSpec((tr, LANE), lambda i: (i, 0)),
        ],
        out_shape=[
            jax.ShapeDtypeStruct((m, n), BF16),
            jax.ShapeDtypeStruct((m, LANE), F32),
        ],
        scratch_shapes=[pltpu.VMEM((tr, d), BF16)],
        compiler_params=pltpu.CompilerParams(
            dimension_semantics=("arbitrary",), vmem_limit_bytes=VMEM_LIMIT),
        name="inproj",
    )(x2, mods4, norm_w, w_main, w_dt)


def _ssd_body(z_ref, xs_ref, bc_ref, pxs_ref, pbc_ref, dt_ref, cw_ref, cb_ref, dtb_ref, alog_ref, dskip_ref, nw_ref,
              o_ref, state_ref, y_ref, *, n_heads, d_ssd, first_chunk):
    gw = SSD_GROUPS * SSD_STATE
    hpg = n_heads // SSD_GROUPS
    gch = hpg * SSD_HEAD_DIM

    srow = lax.broadcasted_iota(jnp.int32, ((SSD_CONV - 1) * CHUNK, 2 * CHUNK), 0)
    scol = lax.broadcasted_iota(jnp.int32, ((SSD_CONV - 1) * CHUNK, 2 * CHUNK), 1)
    src = (srow & (CHUNK - 1)) - (srow // CHUNK + 1) + CHUNK
    keep = (scol >= CHUNK) | jnp.logical_not(first_chunk)
    shift = jnp.where((scol == src) & keep, 1.0, 0.0).astype(BF16)

    def conv_part(cur_ref, prev_ref, lo, hi):
        cur = cur_ref[...]
        sh = _dot(shift, jnp.concatenate([prev_ref[...], cur], axis=0))
        acc = cb_ref[:, lo:hi] + cw_ref[SSD_CONV - 1:SSD_CONV, lo:hi] * cur.astype(F32)
        for dd in range(1, SSD_CONV):
            k = SSD_CONV - 1 - dd
            acc = acc + cw_ref[k:k + 1, lo:hi] * sh[(dd - 1) * CHUNK:dd * CHUNK, :]
        return acc

    conv_xs = conv_part(xs_ref, pxs_ref, 0, d_ssd)
    conv_bc = conv_part(bc_ref, pbc_ref, d_ssd, d_ssd + 2 * gw)
    xs_a = _silu(conv_xs)
    bc_a = _silu(conv_bc)
    bm = bc_a[:, :gw]
    cm = bc_a[:, gw:]

    lane = lax.broadcasted_iota(jnp.int32, (CHUNK, LANE), 1)
    row = lax.broadcasted_iota(jnp.int32, (CHUNK, LANE), 0)
    head_lane = lane < n_heads
    dt_in = dt_ref[...] + dtb_ref[...]
    dt = jnp.maximum(dt_in, 0.0) + jnp.log1p(jnp.exp(-jnp.abs(dt_in)))
    dt = jnp.where(head_lane, dt, 0.0)
    a = dt * (-jnp.exp(alog_ref[...]))
    causal = row >= lane
    tri = jnp.where(causal, 1.0, 0.0).astype(BF16)
    a_cum = _dot_exact_rhs(tri, a)
    a_cum_t = a_cum.T
    ea = jnp.exp(a_cum)
    dec = jnp.exp(a_cum[CHUNK - 1:CHUNK, :] - a_cum)

    erow = lax.broadcasted_iota(jnp.int32, (LANE, d_ssd), 0)
    ecol = lax.broadcasted_iota(jnp.int32, (LANE, d_ssd), 1)
    expand = jnp.where(ecol // SSD_HEAD_DIM == erow, 1.0, 0.0).astype(BF16)
    dt_x = _dot_hi_mid_lhs(dt, expand)
    dtdec_x = _dot_hi_mid_lhs(dt * dec, expand)
    ea_x = _dot_exact_lhs(ea, expand)
    xdt_b = (xs_a * dt_x).astype(BF16)
    xsd_b = (xs_a * dtdec_x).astype(BF16)
    dskip = dskip_ref[...]

    lo_half = lane < SSD_HEAD_DIM
    pairs = hpg // 2
    for g in range(SSD_GROUPS):
        bg = bm[:, g * SSD_STATE:(g + 1) * SSD_STATE].astype(BF16)
        cg = cm[:, g * SSD_STATE:(g + 1) * SSD_STATE].astype(BF16)
        cb = lax.dot_general(cg, bg, (((1,), (1,)), ((), ())), preferred_element_type=F32)
        gsl = slice(g * gch, (g + 1) * gch)
        prev_t = state_ref[g]
        y_off = _dot(cg, prev_t.astype(BF16)) * ea_x[:, gsl]
        states_t = lax.dot_general(bg, xsd_b[:, gsl], (((0,), (0,)), ((), ())),
                                   preferred_element_type=F32)
        state_ref[g] = prev_t * ea_x[CHUNK - 1:CHUNK, gsl] + states_t
        for p in range(pairs):
            t = g * pairs + p
            tsl = slice(t * LANE, (t + 1) * LANE)
            xt = xdt_b[:, tsl]
            yd = None
            for q in range(2):
                h = 2 * t + q
                seg = a_cum[:, h:h + 1] - a_cum_t[h:h + 1, :]
                lmat = jnp.exp(jnp.where(causal, seg, -jnp.inf))
                mh = (cb * lmat).astype(BF16)
                xh = jnp.where(lo_half if q == 0 else jnp.logical_not(lo_half), xt, jnp.zeros_like(xt))
                part = _dot(mh, xh)
                yd = part if yd is None else yd + part
            y_ref[:, tsl] = yd + y_off[:, p * LANE:(p + 1) * LANE] + xs_a[:, tsl] * dskip[:, tsl]

    yz = y_ref[...] * _silu(z_ref[...].astype(F32))
    ms = jnp.mean(yz * yz, axis=-1, keepdims=True)
    o_ref[:, :d_ssd] = (yz * lax.rsqrt(ms + EPS) * nw_ref[...]).astype(BF16)


def _ret_body(q_ref, k_ref, v_ref, g_ref, cos_ref, sin_ref, gw_ref, gb_ref, o_ref, state_ref, qd_ref, kd_ref,
              *, n_heads, log_g, col0):
    dh = RET_HEAD_DIM
    cosf = cos_ref[...]
    sinf = sin_ref[...]
    row = lax.broadcasted_iota(jnp.int32, (CHUNK, CHUNK), 0)
    col = lax.broadcasted_iota(jnp.int32, (CHUNK, CHUNK), 1)
    causal = row >= col
    for h in range(n_heads):
        hs = slice(h * dh, (h + 1) * dh)
        qh = q_ref[:, hs].astype(F32)
        kh = k_ref[:, hs].astype(F32)
        qt = ((qh * cosf + pltpu.roll(qh, dh // 2, 1) * sinf) * qd_ref[h]).astype(BF16)
        kt = ((kh * cosf + pltpu.roll(kh, dh // 2, 1) * sinf) * kd_ref[h]).astype(BF16)
        vh = v_ref[:, hs]
        scores = lax.dot_general(qt, kt, (((1,), (1,)), ((), ())), preferred_element_type=F32)
        inner = _dot(jnp.where(causal, scores, 0.0).astype(BF16), vh)
        prev = state_ref[h]
        cross = _dot(qt, prev.astype(BF16))
        kv = lax.dot_general(kt, vh, (((0,), (0,)), ((), ())), preferred_element_type=F32)
        state_ref[h] = (prev + kv) * math.exp(CHUNK * log_g[h])
        o = inner + cross
        mu = jnp.mean(o, axis=-1, keepdims=True)
        oc = o - mu
        var = jnp.mean(oc * oc, axis=-1, keepdims=True)
        on = oc * lax.rsqrt(var + EPS) * gw_ref[:, hs] + gb_ref[:, hs]
        o_ref[:, col0 + h * dh:col0 + (h + 1) * dh] = (_silu(g_ref[:, hs].astype(F32)) * on).astype(BF16)


def _mixer_kernel(z_ref, xs_ref, bc_ref, pxs_ref, pbc_ref, dt_ref, q_ref, k_ref, v_ref, g_ref, cos_ref, sin_ref,
                  cw_ref, cb_ref, dtb_ref, alog_ref, dskip_ref, nw_ref, gw_ref, gb_ref,
                  o_ref, sstate_ref, y_ref, rstate_ref, qd_ref, kd_ref,
                  *, n_heads, d_ssd, ret_heads, log_g):
    b = pl.program_id(0)
    c = pl.program_id(1)

    @pl.when((b == 0) & (c == 0))
    def _():
        pos = lax.broadcasted_iota(jnp.int32, (CHUNK, RET_HEAD_DIM), 0).astype(F32) + 1.0
        for h in range(ret_heads):
            qd_ref[h] = jnp.exp(pos * log_g[h])
            kd_ref[h] = jnp.exp(pos * (-log_g[h])) * (RET_HEAD_DIM ** -0.5)

    @pl.when(c == 0)
    def _():
        sstate_ref[...] = jnp.zeros(sstate_ref.shape, F32)
        rstate_ref[...] = jnp.zeros(rstate_ref.shape, F32)

    _ssd_body(z_ref, xs_ref, bc_ref, pxs_ref, pbc_ref, dt_ref, cw_ref, cb_ref, dtb_ref, alog_ref, dskip_ref, nw_ref,
              o_ref, sstate_ref, y_ref, n_heads=n_heads, d_ssd=d_ssd, first_chunk=c == 0)
    _ret_body(q_ref, k_ref, v_ref, g_ref, cos_ref, sin_ref, gw_ref, gb_ref, o_ref, rstate_ref, qd_ref, kd_ref,
              n_heads=ret_heads, log_g=log_g, col0=d_ssd)


def _mixer_call(proj, dt_raw, cos_t, sin_t, conv_w, conv_b, dt_bias_p, a_log_p, d_skip_x, ssd_norm_w, gn_w, gn_b,
                *, layer, batch, seq, d_ssd, n_heads, d_ret, ret_heads):
    nc = seq // CHUNK
    l = layer
    depth = conv_w.shape[0]
    xbc_w = conv_w.shape[2]
    bc_w = xbc_w - d_ssd
    bc_blk = (2 * d_ssd + 4 * d_ret) // bc_w
    hpg = n_heads // SSD_GROUPS
    assert d_ssd == d_ret
    log_g = tuple(math.log(1.0 - 2.0 ** (-5.0 - h)) for h in range(ret_heads))
    tok = lambda blk: (lambda b, c: (b * nc + c, blk))
    before = lambda blk: (lambda b, c: (b * nc + jnp.maximum(c - 1, 0), blk))
    par = lambda b, c: (l, 0, 0)
    return pl.pallas_call(
        functools.partial(_mixer_kernel, n_heads=n_heads, d_ssd=d_ssd, ret_heads=ret_heads, log_g=log_g),
        grid=(batch, nc),
        in_specs=[
            pl.BlockSpec((CHUNK, d_ssd), tok(0)),
            pl.BlockSpec((CHUNK, d_ssd), tok(1)),
            pl.BlockSpec((CHUNK, bc_w), tok(bc_blk)),
            pl.BlockSpec((CHUNK, d_ssd), before(1)),
            pl.BlockSpec((CHUNK, bc_w), before(bc_blk)),
            pl.BlockSpec((CHUNK, LANE), tok(0)),
            pl.BlockSpec((CHUNK, d_ret), tok(2)),
            pl.BlockSpec((CHUNK, d_ret), tok(3)),
            pl.BlockSpec((CHUNK, d_ret), tok(4)),
            pl.BlockSpec((CHUNK, d_ret), tok(5)),
            pl.BlockSpec((CHUNK, RET_HEAD_DIM), lambda b, c: (c, 0)),
            pl.BlockSpec((CHUNK, RET_HEAD_DIM), lambda b, c: (c, 0)),
            pl.BlockSpec((None, SSD_CONV, xbc_w), par),
            pl.BlockSpec((None, 1, xbc_w), par),
            pl.BlockSpec((None, 1, LANE), par),
            pl.BlockSpec((None, 1, LANE), par),
            pl.BlockSpec((None, 1, d_ssd), par),
            pl.BlockSpec((None, 1, d_ssd), par),
            pl.BlockSpec((None, 1, d_ret), par),
            pl.BlockSpec((None, 1, d_ret), par),
        ],
        out_specs=pl.BlockSpec((CHUNK, d_ssd + d_ret), lambda b, c: (b * nc + c, 0)),
        out_shape=jax.ShapeDtypeStruct((batch * seq, d_ssd + d_ret), BF16),
        scratch_shapes=[
            pltpu.VMEM((SSD_GROUPS, SSD_STATE, hpg * SSD_HEAD_DIM), F32),
            pltpu.VMEM((CHUNK, d_ssd), F32),
            pltpu.VMEM((ret_heads, RET_HEAD_DIM, RET_HEAD_DIM), F32),
            pltpu.VMEM((ret_heads, CHUNK, RET_HEAD_DIM), F32),
            pltpu.VMEM((ret_heads, CHUNK, RET_HEAD_DIM), F32),
        ],
        compiler_params=pltpu.CompilerParams(
            dimension_semantics=("arbitrary", "arbitrary"), vmem_limit_bytes=VMEM_LIMIT),
        name="mixer",
    )(proj, proj, proj, proj, proj, dt_raw, proj, proj, proj, proj, cos_t, sin_t,
      conv_w, conv_b.reshape(depth, 1, xbc_w), dt_bias_p, a_log_p, d_skip_x,
      ssd_norm_w.reshape(depth, 1, d_ssd), gn_w.reshape(depth, 1, d_ret), gn_b.reshape(depth, 1, d_ret))


def _outproj_kernel(x_ref, mod_ref, y_ref, w_ref, o_ref, wb_ref, *, tn):
    @pl.when(pl.program_id(0) == 0)
    def _():
        def cast(r, carry):
            rows = pl.ds(pl.multiple_of(r * NORM_ROWS, NORM_ROWS), NORM_ROWS)
            wb_ref[rows, :] = w_ref[rows, :].astype(BF16)
            return carry

        lax.fori_loop(0, w_ref.shape[0] // NORM_ROWS, cast, 0)

    y = y_ref[...]
    for n0 in range(0, o_ref.shape[1], tn):
        cols = slice(n0, n0 + tn)
        o_ref[:, cols] = x_ref[:, cols] + mod_ref[5:6, cols] * _dot(y, wb_ref[:, cols])


def _outproj_call(x2, mods4, y_mix, w_out, *, layer, tiles_per_batch, tm):
    m, d = x2.shape
    d_mix = y_mix.shape[1]
    tr = tm // 2
    tpb = tiles_per_batch * 2
    l = layer
    return pl.pallas_call(
        functools.partial(_outproj_kernel, tn=512),
        grid=(m // tr,),
        in_specs=[
            pl.BlockSpec((tr, d), lambda i: (i, 0)),
            pl.BlockSpec((None, None, N_MOD, d), lambda i: (l, i // tpb, 0, 0)),
            pl.BlockSpec((tr, d_mix), lambda i: (i, 0)),
            pl.BlockSpec((None, d_mix, d), lambda i: (l, 0, 0), pipeline_mode=pl.Buffered(1)),
        ],
        out_specs=pl.BlockSpec((tr, d), lambda i: (i, 0)),
        out_shape=jax.ShapeDtypeStruct((m, d), F32),
        scratch_shapes=[pltpu.VMEM((d_mix, d), BF16)],
        compiler_params=pltpu.CompilerParams(
            dimension_semantics=("arbitrary",), vmem_limit_bytes=VMEM_LIMIT),
        name="outproj",
    )(x2, mods4, y_mix, w_out)


def _pad_lanes(v):
    return jnp.pad(v, ((0, 0), (0, LANE - v.shape[1])))[:, None, :]


def kernel(x, c, norm_w, mod_w, mod_b, ffn_w13, ffn_w2, w_in, conv_w, conv_b, dt_bias, a_log, d_skip,
           ssd_norm_w, ret_norm_w, ret_norm_b, w_out, final_norm_w):
    batch, seq, d = x.shape
    depth = norm_w.shape[0]
    n_heads = dt_bias.shape[1]
    d_ssd = ssd_norm_w.shape[1]
    d_ret = ret_norm_w.shape[1]
    ret_heads = d_ret // RET_HEAD_DIM
    xbc_w = conv_w.shape[2]
    m = batch * seq
    tm = min(ROW_TILE, seq)
    tiles_per_batch = seq // tm

    pad_rows = 16
    c_pad = jnp.pad(c, ((0, pad_rows - batch), (0, 0)))
    mods = _mods_call(c_pad, mod_w, mod_b)[:, :batch].reshape(depth, batch, N_MOD, d)

    w_main, w_dt = _regroup_call(w_in.astype(BF16), d_ssd=d_ssd, xbc_w=xbc_w, n_heads=n_heads, d_ret=d_ret)

    half = RET_HEAD_DIM // 2
    inv = ROPE_BASE ** (-jnp.arange(0, RET_HEAD_DIM, 2, dtype=F32) / RET_HEAD_DIM)
    ang = jnp.arange(seq, dtype=F32)[:, None] * inv[None, :]
    cos_t = jnp.concatenate([jnp.cos(ang), jnp.cos(ang)], axis=-1)
    sin_t = jnp.concatenate([-jnp.sin(ang), jnp.sin(ang)], axis=-1)
    assert cos_t.shape == (seq, 2 * half)

    dt_bias_p = _pad_lanes(dt_bias)
    a_log_p = _pad_lanes(a_log)
    d_skip_x = jnp.repeat(d_skip, SSD_HEAD_DIM, axis=1)[:, None, :]
    final_w = final_norm_w.reshape(1, d)

    x2 = x.reshape(m, d)
    common = dict(tiles_per_batch=tiles_per_batch, tm=tm)
    for l in range(depth):
        x2 = _ffn_call(x2, mods, norm_w, final_w, ffn_w13, ffn_w2, layer=l, half=0, sub=0, k0=0,
                       final=False, **common)
        proj, dt_raw = _inproj_call(x2, mods, norm_w, w_main, w_dt, layer=l, **common)
        y_mix = _mixer_call(proj, dt_raw, cos_t, sin_t, conv_w, conv_b, dt_bias_p, a_log_p, d_skip_x, ssd_norm_w,
                            ret_norm_w, ret_norm_b, layer=l, batch=batch, seq=seq, d_ssd=d_ssd,
                            n_heads=n_heads, d_ret=d_ret, ret_heads=ret_heads)
        x2 = _outproj_call(x2, mods, y_mix, w_out, layer=l, **common)
        x2 = _ffn_call(x2, mods, norm_w, final_w, ffn_w13, ffn_w2, layer=l, half=1, sub=2, k0=6,
                       final=(l == depth - 1), **common)
    return x2.reshape(batch, seq, d)
```

```python
import functools
import math

import jax
import jax.numpy as jnp
from jax import lax
from jax.experimental import pallas as pl
from jax.experimental.pallas import tpu as pltpu

F32 = jnp.float32
BF16 = jnp.bfloat16
EPS = 1e-6
LANE = 128
CHUNK = 128
SSD_HEAD_DIM = 64
SSD_GROUPS = 2
SSD_STATE = 128
SSD_CONV = 4
RET_HEAD_DIM = 128
ROPE_BASE = 10000.0
N_MOD = 9
VMEM_LIMIT = 56 * 1024 * 1024
ROW_TILE = 1024
NORM_ROWS = 32


def _silu(x):
    return x * jax.nn.sigmoid(x)


def _split3(x):
    hi = x.astype(BF16)
    r1 = x - hi.astype(F32)
    mid = r1.astype(BF16)
    lo = (r1 - mid.astype(F32)).astype(BF16)
    return hi, mid, lo


def _dot(a, b):
    return jnp.dot(a, b, preferred_element_type=F32)


def _dot_nt(a, b_t):
    return lax.dot_general(a, b_t, (((1,), (1,)), ((), ())), preferred_element_type=F32)


def _dot_exact_rhs(m_bf16, x):
    hi, mid, lo = _split3(x)
    return _dot(m_bf16, hi) + _dot(m_bf16, mid) + _dot(m_bf16, lo)


def _dot_exact_lhs(x, m_bf16):
    hi, mid, lo = _split3(x)
    return _dot(hi, m_bf16) + _dot(mid, m_bf16) + _dot(lo, m_bf16)


def _dot_hi_mid_lhs(x, m_bf16):
    hi = x.astype(BF16)
    mid = (x - hi.astype(F32)).astype(BF16)
    return _dot(hi, m_bf16) + _dot(mid, m_bf16)


def _norm_modulate_rows(x_ref, h_ref, mult, shift, rows_total, copy_ref=None):
    def body(r, carry):
        rows = pl.ds(pl.multiple_of(r * NORM_ROWS, NORM_ROWS), NORM_ROWS)
        xv = x_ref[rows, :]
        ms = jnp.mean(xv * xv, axis=-1, keepdims=True)
        h_ref[rows, :] = (x_ref[rows, :] * lax.rsqrt(ms + EPS) * mult + shift).astype(BF16)
        if copy_ref is not None:
            copy_ref[rows, :] = xv
        return carry

    lax.fori_loop(0, rows_total // NORM_ROWS, body, 0, unroll=2)


def _mods_kernel(c_ref, w_ref, b_ref, o_ref):
    ca = _silu(c_ref[...]).astype(BF16)
    o_ref[...] = _dot(ca, w_ref[...].astype(BF16)) + b_ref[...]


def _mods_call(c_pad, mod_w, mod_b):
    depth, d, n = mod_w.shape
    rows = c_pad.shape[0]
    tn = 2048
    return pl.pallas_call(
        _mods_kernel,
        grid=(depth, n // tn),
        in_specs=[
            pl.BlockSpec((rows, d), lambda l, j: (0, 0)),
            pl.BlockSpec((None, d, tn), lambda l, j: (l, 0, j)),
            pl.BlockSpec((None, 1, tn), lambda l, j: (l, 0, j)),
        ],
        out_specs=pl.BlockSpec((None, rows, tn), lambda l, j: (l, 0, j)),
        out_shape=jax.ShapeDtypeStruct((depth, rows, n), F32),
        compiler_params=pltpu.CompilerParams(
            dimension_semantics=("arbitrary", "arbitrary"), vmem_limit_bytes=VMEM_LIMIT),
        name="mods",
    )(c_pad, mod_w, mod_b.reshape(depth, 1, n))


def _ffn_kernel(x_ref, mod_ref, nw_ref, fw_ref, wa_ref, wb0_ref, wb1_ref, w2_ref,
                o_ref, h_ref, wcat_ref, w2c_ref, *, sub, k0, nj, has_tail, final, tm, d):
    j = pl.program_id(1)

    @pl.when(j == 0)
    def _():
        shift = mod_ref[k0:k0 + 1, :]
        scale = mod_ref[k0 + 1:k0 + 2, :]
        mult = nw_ref[sub:sub + 1, :] * (1.0 + scale)
        _norm_modulate_rows(x_ref, h_ref, mult, shift, tm, copy_ref=o_ref)

    wcat_ref[:, 0 * LANE:2 * LANE] = wa_ref[...].astype(BF16)
    wcat_ref[:, 2 * LANE:3 * LANE] = wb0_ref[...].astype(BF16)
    wcat_ref[:, 3 * LANE:4 * LANE] = wb1_ref[...].astype(BF16)
    ab = _dot(h_ref[...], wcat_ref[...])
    a = ab[:, :2 * LANE]
    b = ab[:, 2 * LANE:]
    g = _silu(a) * b
    w2_hi = w2_ref[LANE:2 * LANE, :]
    if has_tail:
        live = j < nj - 1
        col = lax.broadcasted_iota(jnp.int32, g.shape, 1)
        g = jnp.where((col < LANE) | live, g, 0.0)
        w2_hi = jnp.where(live, w2_hi, 0.0)
    gb = g.astype(BF16)
    w2c_ref[0:LANE, :] = w2_ref[0:LANE, :].astype(BF16)
    w2c_ref[LANE:2 * LANE, :] = w2_hi.astype(BF16)
    nc = 512
    for n0 in range(0, d, nc):
        half_gate = 0.5 * mod_ref[k0 + 2:k0 + 3, n0:n0 + nc]
        o_ref[:, n0:n0 + nc] += half_gate * _dot(gb, w2c_ref[:, n0:n0 + nc])

    if final:
        @pl.when(j == nj - 1)
        def _():
            fw = fw_ref[...]

            def fin(r, carry):
                rows = pl.ds(pl.multiple_of(r * NORM_ROWS, NORM_ROWS), NORM_ROWS)
                xn = o_ref[rows, :]
                ms = jnp.mean(xn * xn, axis=-1, keepdims=True)
                o_ref[rows, :] = xn * lax.rsqrt(ms + EPS) * fw
                return carry

            lax.fori_loop(0, tm // NORM_ROWS, fin, 0, unroll=2)


def _ffn_call(x2, mods4, norm_w, final_w, w13, w2, *, layer, half, sub, k0, final, tiles_per_batch, tm):
    m, d = x2.shape
    d_ff = w2.shape[2]
    nblk = d_ff // LANE
    nj = (nblk + 1) // 2
    has_tail = nblk % 2 == 1
    last = nblk - 1
    l, s = layer, half
    kern = functools.partial(_ffn_kernel, sub=sub, k0=k0, nj=nj, has_tail=has_tail, final=final, tm=tm, d=d)
    return pl.pallas_call(
        kern,
        grid=(m // tm, nj),
        in_specs=[
            pl.BlockSpec((tm, d), lambda i, j: (i, 0)),
            pl.BlockSpec((None, None, N_MOD, d), lambda i, j: (l, i // tiles_per_batch, 0, 0)),
            pl.BlockSpec((None, 3, d), lambda i, j: (l, 0, 0)),
            pl.BlockSpec((1, d), lambda i, j: (0, 0)),
            pl.BlockSpec((None, None, d, 2 * LANE), lambda i, j: (l, s, 0, j)),
            pl.BlockSpec((None, None, d, LANE), lambda i, j: (l, s, 0, nblk + 2 * j)),
            pl.BlockSpec((None, None, d, LANE), lambda i, j: (l, s, 0, nblk + jnp.minimum(2 * j + 1, last))),
            pl.BlockSpec((None, None, 2 * LANE, d), lambda i, j: (l, s, j, 0)),
        ],
        out_specs=pl.BlockSpec((tm, d), lambda i, j: (i, 0)),
        out_shape=jax.ShapeDtypeStruct((m, d), F32),
        scratch_shapes=[
            pltpu.VMEM((tm, d), BF16),
            pltpu.VMEM((d, 4 * LANE), BF16),
            pltpu.VMEM((2 * LANE, d), BF16),
        ],
        compiler_params=pltpu.CompilerParams(
            dimension_semantics=("arbitrary", "arbitrary"), vmem_limit_bytes=VMEM_LIMIT),
        name=f"ffn_{sub}",
    )(x2, mods4, norm_w, final_w, w13, w13, w13, w2)


def _inproj_kernel(x_ref, mod_ref, nw_ref, w_ref, o_ref, dt_ref, h_ref, *, tm, tn, segments, dt_row):
    shift = mod_ref[3:4, :]
    scale = mod_ref[4:5, :]
    mult = nw_ref[1:2, :] * (1.0 + scale)
    _norm_modulate_rows(x_ref, h_ref, mult, shift, tm)
    h = h_ref[...]
    dt_ref[...] = _dot_nt(h, w_ref[dt_row:dt_row + LANE, :])
    for col0, row0, width in segments:
        for n0 in range(0, width, tn):
            o_ref[:, col0 + n0:col0 + n0 + tn] = _dot_nt(h, w_ref[row0 + n0:row0 + n0 + tn, :]).astype(BF16)


def _inproj_call(x2, mods4, norm_w, w_t, *, layer, tiles_per_batch, tm, d_ssd, xbc_w, n_heads, d_ret):
    m, d = x2.shape
    n_in = w_t.shape[1]
    tr = tm // 2
    tpb = tiles_per_batch * 2
    l = layer
    o_dt = d_ssd + xbc_w
    o_q = o_dt + n_heads
    bc_w = xbc_w - d_ssd
    segments = ((0, 0, 2 * d_ssd), (2 * d_ssd, o_q, 4 * d_ret), (2 * d_ssd + 4 * d_ret, 2 * d_ssd, bc_w))
    n = 2 * d_ssd + 4 * d_ret + bc_w
    assert all(r % 16 == 0 and w % 512 == 0 for _, r, w in segments) and o_dt + LANE <= n_in
    return pl.pallas_call(
        functools.partial(_inproj_kernel, tm=tr, tn=512, segments=segments, dt_row=o_dt),
        grid=(m // tr,),
        in_specs=[
            pl.BlockSpec((tr, d), lambda i: (i, 0)),
            pl.BlockSpec((None, None, N_MOD, d), lambda i: (l, i // tpb, 0, 0)),
            pl.BlockSpec((None, 3, d), lambda i: (l, 0, 0)),
            pl.BlockSpec((None, n_in, d), lambda i: (l, 0, 0), pipeline_mode=pl.Buffered(1)),
        ],
        out_specs=[
            pl.BlockSpec((tr, n), lambda i: (i, 0)),
            pl.BlockSpec((tr, LANE), lambda i: (i, 0)),
        ],
        out_shape=[
            jax.ShapeDtypeStruct((m, n), BF16),
            jax.ShapeDtypeStruct((m, LANE), F32),
        ],
        scratch_shapes=[pltpu.VMEM((tr, d), BF16)],
        compiler_params=pltpu.CompilerParams(
            dimension_semantics=("arbitrary",), vmem_limit_bytes=VMEM_LIMIT),
        name="inproj",
    )(x2, mods4, norm_w, w_t)


def _ssd_body(z_ref, xs_ref, bc_ref, pxs_ref, pbc_ref, dt_ref, cw_ref, cb_ref, dtb_ref, alog_ref, dskip_ref, nw_ref,
              o_ref, state_ref, y_ref, *, n_heads, d_ssd, first_chunk):
    gw = SSD_GROUPS * SSD_STATE
    hpg = n_heads // SSD_GROUPS
    gch = hpg * SSD_HEAD_DIM

    srow = lax.broadcasted_iota(jnp.int32, ((SSD_CONV - 1) * CHUNK, 2 * CHUNK), 0)
    scol = lax.broadcasted_iota(jnp.int32, ((SSD_CONV - 1) * CHUNK, 2 * CHUNK), 1)
    src = (srow & (CHUNK - 1)) - (srow // CHUNK + 1) + CHUNK
    keep = (scol >= CHUNK) | jnp.logical_not(first_chunk)
    shift = jnp.where((scol == src) & keep, 1.0, 0.0).astype(BF16)

    def conv_part(cur_ref, prev_ref, lo, hi):
        cur = cur_ref[...]
        sh = _dot(shift, jnp.concatenate([prev_ref[...], cur], axis=0))
        acc = cb_ref[:, lo:hi] + cw_ref[SSD_CONV - 1:SSD_CONV, lo:hi] * cur.astype(F32)
        for dd in range(1, SSD_CONV):
            k = SSD_CONV - 1 - dd
            acc = acc + cw_ref[k:k + 1, lo:hi] * sh[(dd - 1) * CHUNK:dd * CHUNK, :]
        return acc

    conv_xs = conv_part(xs_ref, pxs_ref, 0, d_ssd)
    conv_bc = conv_part(bc_ref, pbc_ref, d_ssd, d_ssd + 2 * gw)
    xs_a = _silu(conv_xs)
    bc_a = _silu(conv_bc)
    bm = bc_a[:, :gw]
    cm = bc_a[:, gw:]

    lane = lax.broadcasted_iota(jnp.int32, (CHUNK, LANE), 1)
    row = lax.broadcasted_iota(jnp.int32, (CHUNK, LANE), 0)
    head_lane = lane < n_heads
    dt_in = dt_ref[...] + dtb_ref[...]
    dt = jnp.maximum(dt_in, 0.0) + jnp.log1p(jnp.exp(-jnp.abs(dt_in)))
    dt = jnp.where(head_lane, dt, 0.0)
    a = dt * (-jnp.exp(alog_ref[...]))
    causal = row >= lane
    tri = jnp.where(causal, 1.0, 0.0).astype(BF16)
    a_cum = _dot_exact_rhs(tri, a)
    a_cum_t = a_cum.T
    ea = jnp.exp(a_cum)
    dec = jnp.exp(a_cum[CHUNK - 1:CHUNK, :] - a_cum)

    erow = lax.broadcasted_iota(jnp.int32, (LANE, d_ssd), 0)
    ecol = lax.broadcasted_iota(jnp.int32, (LANE, d_ssd), 1)
    expand = jnp.where(ecol // SSD_HEAD_DIM == erow, 1.0, 0.0).astype(BF16)
    dt_x = _dot_hi_mid_lhs(dt, expand)
    dtdec_x = _dot_hi_mid_lhs(dt * dec, expand)
    ea_x = _dot_exact_lhs(ea, expand)
    xdt_b = (xs_a * dt_x).astype(BF16)
    xsd_b = (xs_a * dtdec_x).astype(BF16)
    dskip = dskip_ref[...]

    lo_half = lane < SSD_HEAD_DIM
    pairs = hpg // 2
    for g in range(SSD_GROUPS):
        bg = bm[:, g * SSD_STATE:(g + 1) * SSD_STATE].astype(BF16)
        cg = cm[:, g * SSD_STATE:(g + 1) * SSD_STATE].astype(BF16)
        cb = lax.dot_general(cg, bg, (((1,), (1,)), ((), ())), preferred_element_type=F32)
        gsl = slice(g * gch, (g + 1) * gch)
        prev_t = state_ref[g]
        y_off = _dot(cg, prev_t.astype(BF16)) * ea_x[:, gsl]
        states_t = lax.dot_general(bg, xsd_b[:, gsl], (((0,), (0,)), ((), ())),
                                   preferred_element_type=F32)
        state_ref[g] = prev_t * ea_x[CHUNK - 1:CHUNK, gsl] + states_t
        for p in range(pairs):
            t = g * pairs + p
            tsl = slice(t * LANE, (t + 1) * LANE)
            xt = xdt_b[:, tsl]
            yd = None
            for q in range(2):
                h = 2 * t + q
                seg = a_cum[:, h:h + 1] - a_cum_t[h:h + 1, :]
                lmat = jnp.exp(jnp.where(causal, seg, -jnp.inf))
                mh = (cb * lmat).astype(BF16)
                xh = jnp.where(lo_half if q == 0 else jnp.logical_not(lo_half), xt, jnp.zeros_like(xt))
                part = _dot(mh, xh)
                yd = part if yd is None else yd + part
            y_ref[:, tsl] = yd + y_off[:, p * LANE:(p + 1) * LANE] + xs_a[:, tsl] * dskip[:, tsl]

    yz = y_ref[...] * _silu(z_ref[...].astype(F32))
    ms = jnp.mean(yz * yz, axis=-1, keepdims=True)
    o_ref[:, :d_ssd] = (yz * lax.rsqrt(ms + EPS) * nw_ref[...]).astype(BF16)


def _ret_body(q_ref, k_ref, v_ref, g_ref, cos_ref, sin_ref, gw_ref, gb_ref, o_ref, state_ref, qd_ref, kd_ref,
              *, n_heads, log_g, col0):
    dh = RET_HEAD_DIM
    cosf = cos_ref[...]
    sinf = sin_ref[...]
    row = lax.broadcasted_iota(jnp.int32, (CHUNK, CHUNK), 0)
    col = lax.broadcasted_iota(jnp.int32, (CHUNK, CHUNK), 1)
    causal = row >= col
    for h in range(n_heads):
        hs = slice(h * dh, (h + 1) * dh)
        qh = q_ref[:, hs].astype(F32)
        kh = k_ref[:, hs].astype(F32)
        qt = ((qh * cosf + pltpu.roll(qh, dh // 2, 1) * sinf) * qd_ref[h]).astype(BF16)
        kt = ((kh * cosf + pltpu.roll(kh, dh // 2, 1) * sinf) * kd_ref[h]).astype(BF16)
        vh = v_ref[:, hs]
        scores = lax.dot_general(qt, kt, (((1,), (1,)), ((), ())), preferred_element_type=F32)
        inner = _dot(jnp.where(causal, scores, 0.0).astype(BF16), vh)
        prev = state_ref[h]
        cross = _dot(qt, prev.astype(BF16))
        kv = lax.dot_general(kt, vh, (((0,), (0,)), ((), ())), preferred_element_type=F32)
        state_ref[h] = (prev + kv) * math.exp(CHUNK * log_g[h])
        o = inner + cross
        mu = jnp.mean(o, axis=-1, keepdims=True)
        oc = o - mu
        var = jnp.mean(oc * oc, axis=-1, keepdims=True)
        on = oc * lax.rsqrt(var + EPS) * gw_ref[:, hs] + gb_ref[:, hs]
        o_ref[:, col0 + h * dh:col0 + (h + 1) * dh] = (_silu(g_ref[:, hs].astype(F32)) * on).astype(BF16)


def _mixer_kernel(z_ref, xs_ref, bc_ref, pxs_ref, pbc_ref, dt_ref, q_ref, k_ref, v_ref, g_ref, cos_ref, sin_ref,
                  cw_ref, cb_ref, dtb_ref, alog_ref, dskip_ref, nw_ref, gw_ref, gb_ref,
                  o_ref, sstate_ref, y_ref, rstate_ref, qd_ref, kd_ref,
                  *, n_heads, d_ssd, ret_heads, log_g):
    b = pl.program_id(0)
    c = pl.program_id(1)

    @pl.when((b == 0) & (c == 0))
    def _():
        pos = lax.broadcasted_iota(jnp.int32, (CHUNK, RET_HEAD_DIM), 0).astype(F32) + 1.0
        for h in range(ret_heads):
            qd_ref[h] = jnp.exp(pos * log_g[h])
            kd_ref[h] = jnp.exp(pos * (-log_g[h])) * (RET_HEAD_DIM ** -0.5)

    @pl.when(c == 0)
    def _():
        sstate_ref[...] = jnp.zeros(sstate_ref.shape, F32)
        rstate_ref[...] = jnp.zeros(rstate_ref.shape, F32)

    _ssd_body(z_ref, xs_ref, bc_ref, pxs_ref, pbc_ref, dt_ref, cw_ref, cb_ref, dtb_ref, alog_ref, dskip_ref, nw_ref,
              o_ref, sstate_ref, y_ref, n_heads=n_heads, d_ssd=d_ssd, first_chunk=c == 0)
    _ret_body(q_ref, k_ref, v_ref, g_ref, cos_ref, sin_ref, gw_ref, gb_ref, o_ref, rstate_ref, qd_ref, kd_ref,
              n_heads=ret_heads, log_g=log_g, col0=d_ssd)


def _mixer_call(proj, dt_raw, cos_t, sin_t, conv_w, conv_b, dt_bias_p, a_log_p, d_skip_x, ssd_norm_w, gn_w, gn_b,
                *, layer, batch, seq, d_ssd, n_heads, d_ret, ret_heads):
    nc = seq // CHUNK
    l = layer
    depth = conv_w.shape[0]
    xbc_w = conv_w.shape[2]
    bc_w = xbc_w - d_ssd
    bc_blk = (2 * d_ssd + 4 * d_ret) // bc_w
    hpg = n_heads // SSD_GROUPS
    assert d_ssd == d_ret
    log_g = tuple(math.log(1.0 - 2.0 ** (-5.0 - h)) for h in range(ret_heads))
    tok = lambda blk: (lambda b, c: (b * nc + c, blk))
    before = lambda blk: (lambda b, c: (b * nc + jnp.maximum(c - 1, 0), blk))
    par = lambda b, c: (l, 0, 0)
    return pl.pallas_call(
        functools.partial(_mixer_kernel, n_heads=n_heads, d_ssd=d_ssd, ret_heads=ret_heads, log_g=log_g),
        grid=(batch, nc),
        in_specs=[
            pl.BlockSpec((CHUNK, d_ssd), tok(0)),
            pl.BlockSpec((CHUNK, d_ssd), tok(1)),
            pl.BlockSpec((CHUNK, bc_w), tok(bc_blk)),
            pl.BlockSpec((CHUNK, d_ssd), before(1)),
            pl.BlockSpec((CHUNK, bc_w), before(bc_blk)),
            pl.BlockSpec((CHUNK, LANE), tok(0)),
            pl.BlockSpec((CHUNK, d_ret), tok(2)),
            pl.BlockSpec((CHUNK, d_ret), tok(3)),
            pl.BlockSpec((CHUNK, d_ret), tok(4)),
            pl.BlockSpec((CHUNK, d_ret), tok(5)),
            pl.BlockSpec((CHUNK, RET_HEAD_DIM), lambda b, c: (c, 0)),
            pl.BlockSpec((CHUNK, RET_HEAD_DIM), lambda b, c: (c, 0)),
            pl.BlockSpec((None, SSD_CONV, xbc_w), par),
            pl.BlockSpec((None, 1, xbc_w), par),
            pl.BlockSpec((None, 1, LANE), par),
            pl.BlockSpec((None, 1, LANE), par),
            pl.BlockSpec((None, 1, d_ssd), par),
            pl.BlockSpec((None, 1, d_ssd), par),
            pl.BlockSpec((None, 1, d_ret), par),
            pl.BlockSpec((None, 1, d_ret), par),
        ],
        out_specs=pl.BlockSpec((CHUNK, d_ssd + d_ret), lambda b, c: (b * nc + c, 0)),
        out_shape=jax.ShapeDtypeStruct((batch * seq, d_ssd + d_ret), BF16),
        scratch_shapes=[
            pltpu.VMEM((SSD_GROUPS, SSD_STATE, hpg * SSD_HEAD_DIM), F32),
            pltpu.VMEM((CHUNK, d_ssd), F32),
            pltpu.VMEM((ret_heads, RET_HEAD_DIM, RET_HEAD_DIM), F32),
            pltpu.VMEM((ret_heads, CHUNK, RET_HEAD_DIM), F32),
            pltpu.VMEM((ret_heads, CHUNK, RET_HEAD_DIM), F32),
        ],
        compiler_params=pltpu.CompilerParams(
            dimension_semantics=("arbitrary", "arbitrary"), vmem_limit_bytes=VMEM_LIMIT),
        name="mixer",
    )(proj, proj, proj, proj, proj, dt_raw, proj, proj, proj, proj, cos_t, sin_t,
      conv_w, conv_b.reshape(depth, 1, xbc_w), dt_bias_p, a_log_p, d_skip_x,
      ssd_norm_w.reshape(depth, 1, d_ssd), gn_w.reshape(depth, 1, d_ret), gn_b.reshape(depth, 1, d_ret))


def _outproj_kernel(x_ref, mod_ref, y_ref, w_ref, o_ref, wb_ref, *, tn):
    @pl.when(pl.program_id(0) == 0)
    def _():
        def cast(r, carry):
            rows = pl.ds(pl.multiple_of(r * NORM_ROWS, NORM_ROWS), NORM_ROWS)
            wb_ref[rows, :] = w_ref[rows, :].astype(BF16)
            return carry

        lax.fori_loop(0, w_ref.shape[0] // NORM_ROWS, cast, 0)

    y = y_ref[...]
    for n0 in range(0, o_ref.shape[1], tn):
        cols = slice(n0, n0 + tn)
        o_ref[:, cols] = x_ref[:, cols] + mod_ref[5:6, cols] * _dot(y, wb_ref[:, cols])


def _outproj_call(x2, mods4, y_mix, w_out, *, layer, tiles_per_batch, tm):
    m, d = x2.shape
    d_mix = y_mix.shape[1]
    tr = tm // 2
    tpb = tiles_per_batch * 2
    l = layer
    return pl.pallas_call(
        functools.partial(_outproj_kernel, tn=512),
        grid=(m // tr,),
        in_specs=[
            pl.BlockSpec((tr, d), lambda i: (i, 0)),
            pl.BlockSpec((None, None, N_MOD, d), lambda i: (l, i // tpb, 0, 0)),
            pl.BlockSpec((tr, d_mix), lambda i: (i, 0)),
            pl.BlockSpec((None, d_mix, d), lambda i: (l, 0, 0), pipeline_mode=pl.Buffered(1)),
        ],
        out_specs=pl.BlockSpec((tr, d), lambda i: (i, 0)),
        out_shape=jax.ShapeDtypeStruct((m, d), F32),
        scratch_shapes=[pltpu.VMEM((d_mix, d), BF16)],
        compiler_params=pltpu.CompilerParams(
            dimension_semantics=("arbitrary",), vmem_limit_bytes=VMEM_LIMIT),
        name="outproj",
    )(x2, mods4, y_mix, w_out)


def _pad_lanes(v):
    return jnp.pad(v, ((0, 0), (0, LANE - v.shape[1])))[:, None, :]


def kernel(x, c, norm_w, mod_w, mod_b, ffn_w13, ffn_w2, w_in, conv_w, conv_b, dt_bias, a_log, d_skip,
           ssd_norm_w, ret_norm_w, ret_norm_b, w_out, final_norm_w):
    batch, seq, d = x.shape
    depth = norm_w.shape[0]
    n_heads = dt_bias.shape[1]
    d_ssd = ssd_norm_w.shape[1]
    d_ret = ret_norm_w.shape[1]
    ret_heads = d_ret // RET_HEAD_DIM
    xbc_w = conv_w.shape[2]
    m = batch * seq
    tm = min(ROW_TILE, seq)
    tiles_per_batch = seq // tm

    pad_rows = 16
    c_pad = jnp.pad(c, ((0, pad_rows - batch), (0, 0)))
    mods = _mods_call(c_pad, mod_w, mod_b)[:, :batch].reshape(depth, batch, N_MOD, d)

    w_t = jnp.swapaxes(w_in, 1, 2).astype(BF16)

    half = RET_HEAD_DIM // 2
    inv = ROPE_BASE ** (-jnp.arange(0, RET_HEAD_DIM, 2, dtype=F32) / RET_HEAD_DIM)
    ang = jnp.arange(seq, dtype=F32)[:, None] * inv[None, :]
    cos_t = jnp.concatenate([jnp.cos(ang), jnp.cos(ang)], axis=-1)
    sin_t = jnp.concatenate([-jnp.sin(ang), jnp.sin(ang)], axis=-1)
    assert cos_t.shape == (seq, 2 * half)

    dt_bias_p = _pad_lanes(dt_bias)
    a_log_p = _pad_lanes(a_log)
    d_skip_x = jnp.repeat(d_skip, SSD_HEAD_DIM, axis=1)[:, None, :]
    final_w = final_norm_w.reshape(1, d)

    x2 = x.reshape(m, d)
    common = dict(tiles_per_batch=tiles_per_batch, tm=tm)
    for l in range(depth):
        x2 = _ffn_call(x2, mods, norm_w, final_w, ffn_w13, ffn_w2, layer=l, half=0, sub=0, k0=0,
                       final=False, **common)
        proj, dt_raw = _inproj_call(x2, mods, norm_w, w_t, layer=l, d_ssd=d_ssd, xbc_w=xbc_w, n_heads=n_heads,
                                    d_ret=d_ret, **common)
        y_mix = _mixer_call(proj, dt_raw, cos_t, sin_t, conv_w, conv_b, dt_bias_p, a_log_p, d_skip_x, ssd_norm_w,
                            ret_norm_w, ret_norm_b, layer=l, batch=batch, seq=seq, d_ssd=d_ssd,
                            n_heads=n_heads, d_ret=d_ret, ret_heads=ret_heads)
        x2 = _outproj_call(x2, mods, y_mix, w_out, layer=l, **common)
        x2 = _ffn_call(x2, mods, norm_w, final_w, ffn_w13, ffn_w2, layer=l, half=1, sub=2, k0=6,
                       final=(l == depth - 1), **common)
    return x2.reshape(batch, seq, d)
```

```python
import functools
import math

import jax
import jax.numpy as jnp
from jax import lax
from jax.experimental import pallas as pl
from jax.experimental.pallas import tpu as pltpu

F32 = jnp.float32
BF16 = jnp.bfloat16
EPS = 1e-6
LANE = 128
CHUNK = 128
SSD_HEAD_DIM = 64
SSD_GROUPS = 2
SSD_STATE = 128
SSD_CONV = 4
RET_HEAD_DIM = 128
ROPE_BASE = 10000.0
N_MOD = 9
VMEM_LIMIT = 56 * 1024 * 1024
ROW_TILE = 1024
NORM_ROWS = 32
CAST_ROWS = 48


def _silu(x):
    return x * jax.nn.sigmoid(x)


def _split3(x):
    hi = x.astype(BF16)
    r1 = x - hi.astype(F32)
    mid = r1.astype(BF16)
    lo = (r1 - mid.astype(F32)).astype(BF16)
    return hi, mid, lo


def _dot(a, b):
    return jnp.dot(a, b, preferred_element_type=F32)


def _dot_nt(a, b_t):
    return lax.dot_general(a, b_t, (((1,), (1,)), ((), ())), preferred_element_type=F32)


def _dot_exact_rhs(m_bf16, x):
    hi, mid, lo = _split3(x)
    return _dot(m_bf16, hi) + _dot(m_bf16, mid) + _dot(m_bf16, lo)


def _dot_exact_lhs(x, m_bf16):
    hi, mid, lo = _split3(x)
    return _dot(hi, m_bf16) + _dot(mid, m_bf16) + _dot(lo, m_bf16)


def _dot_hi_mid_lhs(x, m_bf16):
    hi = x.astype(BF16)
    mid = (x - hi.astype(F32)).astype(BF16)
    return _dot(hi, m_bf16) + _dot(mid, m_bf16)


def _norm_modulate_rows(x_ref, h_ref, mult, shift, rows_total, copy_ref=None):
    def body(r, carry):
        rows = pl.ds(pl.multiple_of(r * NORM_ROWS, NORM_ROWS), NORM_ROWS)
        xv = x_ref[rows, :]
        ms = jnp.mean(xv * xv, axis=-1, keepdims=True)
        h_ref[rows, :] = (x_ref[rows, :] * lax.rsqrt(ms + EPS) * mult + shift).astype(BF16)
        if copy_ref is not None:
            copy_ref[rows, :] = xv
        return carry

    lax.fori_loop(0, rows_total // NORM_ROWS, body, 0, unroll=4)


def _mods_kernel(c_ref, w_ref, b_ref, o_ref):
    ca = _silu(c_ref[...]).astype(BF16)
    o_ref[...] = _dot(ca, w_ref[...].astype(BF16)) + b_ref[...]


def _mods_call(c_pad, mod_w, mod_b):
    depth, d, n = mod_w.shape
    rows = c_pad.shape[0]
    tn = 2048
    return pl.pallas_call(
        _mods_kernel,
        grid=(depth, n // tn),
        in_specs=[
            pl.BlockSpec((rows, d), lambda l, j: (0, 0)),
            pl.BlockSpec((None, d, tn), lambda l, j: (l, 0, j)),
            pl.BlockSpec((None, 1, tn), lambda l, j: (l, 0, j)),
        ],
        out_specs=pl.BlockSpec((None, rows, tn), lambda l, j: (l, 0, j)),
        out_shape=jax.ShapeDtypeStruct((depth, rows, n), F32),
        compiler_params=pltpu.CompilerParams(
            dimension_semantics=("arbitrary", "arbitrary"), vmem_limit_bytes=VMEM_LIMIT),
        name="mods",
    )(c_pad, mod_w, mod_b.reshape(depth, 1, n))


def _ffn_kernel(x_ref, mod_ref, nw_ref, fw_ref, wa_ref, wb0_ref, wb1_ref, w2_ref, *rest,
                sub, k0, nj, has_tail, final, tm, d, side_cast):
    if side_cast:
        src_ref, o_ref, dst_ref, h_ref, wcat_ref, w2c_ref = rest
        dst_ref[...] = src_ref[...].astype(BF16)
    else:
        o_ref, h_ref, wcat_ref, w2c_ref = rest
    j = pl.program_id(1)

    @pl.when(j == 0)
    def _():
        shift = mod_ref[k0:k0 + 1, :]
        scale = mod_ref[k0 + 1:k0 + 2, :]
        mult = nw_ref[sub:sub + 1, :] * (1.0 + scale)
        _norm_modulate_rows(x_ref, h_ref, mult, shift, tm, copy_ref=o_ref)

    wcat_ref[:, 0 * LANE:2 * LANE] = wa_ref[...].astype(BF16)
    wcat_ref[:, 2 * LANE:3 * LANE] = wb0_ref[...].astype(BF16)
    wcat_ref[:, 3 * LANE:4 * LANE] = wb1_ref[...].astype(BF16)
    ab = _dot(h_ref[...], wcat_ref[...])
    a = ab[:, :2 * LANE]
    b = ab[:, 2 * LANE:]
    g = _silu(a) * b
    w2_hi = w2_ref[LANE:2 * LANE, :]
    if has_tail:
        live = j < nj - 1
        col = lax.broadcasted_iota(jnp.int32, g.shape, 1)
        g = jnp.where((col < LANE) | live, g, 0.0)
        w2_hi = jnp.where(live, w2_hi, 0.0)
    gb = g.astype(BF16)
    w2c_ref[0:LANE, :] = w2_ref[0:LANE, :].astype(BF16)
    w2c_ref[LANE:2 * LANE, :] = w2_hi.astype(BF16)
    nc = 512
    for n0 in range(0, d, nc):
        half_gate = 0.5 * mod_ref[k0 + 2:k0 + 3, n0:n0 + nc]
        o_ref[:, n0:n0 + nc] += half_gate * _dot(gb, w2c_ref[:, n0:n0 + nc])

    if final:
        @pl.when(j == nj - 1)
        def _():
            fw = fw_ref[...]

            def fin(r, carry):
                rows = pl.ds(pl.multiple_of(r * NORM_ROWS, NORM_ROWS), NORM_ROWS)
                xn = o_ref[rows, :]
                ms = jnp.mean(xn * xn, axis=-1, keepdims=True)
                o_ref[rows, :] = xn * lax.rsqrt(ms + EPS) * fw
                return carry

            lax.fori_loop(0, tm // NORM_ROWS, fin, 0, unroll=2)


def _ffn_call(x2, mods4, norm_w, final_w, w13, w2, *, layer, half, sub, k0, final, tiles_per_batch, tm,
              cast_src=None):
    m, d = x2.shape
    d_ff = w2.shape[2]
    nblk = d_ff // LANE
    nj = (nblk + 1) // 2
    has_tail = nblk % 2 == 1
    last = nblk - 1
    l, s = layer, half
    kern = functools.partial(_ffn_kernel, sub=sub, k0=k0, nj=nj, has_tail=has_tail, final=final, tm=tm, d=d,
                             side_cast=cast_src is not None)
    extra_in, extra_out_spec, extra_out_shape, extra_args = [], [], [], []
    if cast_src is not None and (cast_src.shape[1] // CAST_ROWS) > (m // tm) * nj:
        return _ffn_call(x2, mods4, norm_w, final_w, w13, w2, layer=layer, half=half, sub=sub, k0=k0, final=final,
                         tiles_per_batch=tiles_per_batch, tm=tm), cast_src[layer].astype(BF16)
    if cast_src is not None:
        rows = cast_src.shape[1]
        n_cast = rows // CAST_ROWS
        assert rows % CAST_ROWS == 0 and cast_src.shape[2] == d
        cast_blk = lambda i, j: jnp.minimum(i * nj + j, n_cast - 1)
        extra_in = [pl.BlockSpec((None, CAST_ROWS, d), lambda i, j: (l, cast_blk(i, j), 0))]
        extra_out_spec = [pl.BlockSpec((CAST_ROWS, d), lambda i, j: (cast_blk(i, j), 0))]
        extra_out_shape = [jax.ShapeDtypeStruct((rows, d), BF16)]
        extra_args = [cast_src]
    out = pl.pallas_call(
        kern,
        grid=(m // tm, nj),
        in_specs=[
            pl.BlockSpec((tm, d), lambda i, j: (i, 0)),
            pl.BlockSpec((None, None, N_MOD, d), lambda i, j: (l, i // tiles_per_batch, 0, 0)),
            pl.BlockSpec((None, 3, d), lambda i, j: (l, 0, 0)),
            pl.BlockSpec((1, d), lambda i, j: (0, 0)),
            pl.BlockSpec((None, None, d, 2 * LANE), lambda i, j: (l, s, 0, j)),
            pl.BlockSpec((None, None, d, LANE), lambda i, j: (l, s, 0, nblk + 2 * j)),
            pl.BlockSpec((None, None, d, LANE), lambda i, j: (l, s, 0, nblk + jnp.minimum(2 * j + 1, last))),
            pl.BlockSpec((None, None, 2 * LANE, d), lambda i, j: (l, s, j, 0)),
        ] + extra_in,
        out_specs=[pl.BlockSpec((tm, d), lambda i, j: (i, 0))] + extra_out_spec,
        out_shape=[jax.ShapeDtypeStruct((m, d), F32)] + extra_out_shape,
        scratch_shapes=[
            pltpu.VMEM((tm, d), BF16),
            pltpu.VMEM((d, 4 * LANE), BF16),
            pltpu.VMEM((2 * LANE, d), BF16),
        ],
        compiler_params=pltpu.CompilerParams(
            dimension_semantics=("arbitrary", "arbitrary"), vmem_limit_bytes=VMEM_LIMIT),
        name=f"ffn_{sub}",
    )(x2, mods4, norm_w, final_w, w13, w13, w13, w2, *extra_args)
    return out if cast_src is not None else out[0]


def _inproj_kernel(x_ref, mod_ref, nw_ref, w_ref, o_ref, dt_ref, h_ref, *, tm, tn, segments, dt_row):
    shift = mod_ref[3:4, :]
    scale = mod_ref[4:5, :]
    mult = nw_ref[1:2, :] * (1.0 + scale)
    _norm_modulate_rows(x_ref, h_ref, mult, shift, tm)
    h = h_ref[...]
    dt_ref[...] = _dot_nt(h, w_ref[dt_row:dt_row + LANE, :])
    for col0, row0, width in segments:
        for n0 in range(0, width, tn):
            o_ref[:, col0 + n0:col0 + n0 + tn] = _dot_nt(h, w_ref[row0 + n0:row0 + n0 + tn, :]).astype(BF16)


def _inproj_call(x2, mods4, norm_w, w_t, *, layer, tiles_per_batch, tm, d_ssd, xbc_w, n_heads, d_ret):
    m, d = x2.shape
    n_in = w_t.shape[0]
    tr = tm // 2
    tpb = tiles_per_batch * 2
    l = layer
    o_dt = d_ssd + xbc_w
    o_q = o_dt + n_heads
    bc_w = xbc_w - d_ssd
    segments = ((0, 0, 2 * d_ssd), (2 * d_ssd, o_q, 4 * d_ret), (2 * d_ssd + 4 * d_ret, 2 * d_ssd, bc_w))
    n = 2 * d_ssd + 4 * d_ret + bc_w
    assert all(r % 16 == 0 and w % 512 == 0 for _, r, w in segments) and o_dt + LANE <= n_in
    return pl.pallas_call(
        functools.partial(_inproj_kernel, tm=tr, tn=512, segments=segments, dt_row=o_dt),
        grid=(m // tr,),
        in_specs=[
            pl.BlockSpec((tr, d), lambda i: (i, 0)),
            pl.BlockSpec((None, None, N_MOD, d), lambda i: (l, i // tpb, 0, 0)),
            pl.BlockSpec((None, 3, d), lambda i: (l, 0, 0)),
            pl.BlockSpec((n_in, d), lambda i: (0, 0), pipeline_mode=pl.Buffered(1)),
        ],
        out_specs=[
            pl.BlockSpec((tr, n), lambda i: (i, 0)),
            pl.BlockSpec((tr, LANE), lambda i: (i, 0)),
        ],
        out_shape=[
            jax.ShapeDtypeStruct((m, n), BF16),
            jax.ShapeDtypeStruct((m, LANE), F32),
        ],
        scratch_shapes=[pltpu.VMEM((tr, d), BF16)],
        compiler_params=pltpu.CompilerParams(
            dimension_semantics=("arbitrary",), vmem_limit_bytes=VMEM_LIMIT),
        name="inproj",
    )(x2, mods4, norm_w, w_t)


def _ssd_body(z_ref, xs_ref, bc_ref, pxs_ref, pbc_ref, dt_ref, cw_ref, cb_ref, dtb_ref, alog_ref, dskip_ref, nw_ref,
              o_ref, state_ref, y_ref, *, n_heads, d_ssd, first_chunk):
    gw = SSD_GROUPS * SSD_STATE
    hpg = n_heads // SSD_GROUPS
    gch = hpg * SSD_HEAD_DIM

    srow = lax.broadcasted_iota(jnp.int32, ((SSD_CONV - 1) * CHUNK, 2 * CHUNK), 0)
    scol = lax.broadcasted_iota(jnp.int32, ((SSD_CONV - 1) * CHUNK, 2 * CHUNK), 1)
    src = (srow & (CHUNK - 1)) - (srow // CHUNK + 1) + CHUNK
    keep = (scol >= CHUNK) | jnp.logical_not(first_chunk)
    shift = jnp.where((scol == src) & keep, 1.0, 0.0).astype(BF16)

    def conv_part(cur_ref, prev_ref, lo, hi):
        cur = cur_ref[...]
        sh = _dot(shift, jnp.concatenate([prev_ref[...], cur], axis=0))
        acc = cb_ref[:, lo:hi] + cw_ref[SSD_CONV - 1:SSD_CONV, lo:hi] * cur.astype(F32)
        for dd in range(1, SSD_CONV):
            k = SSD_CONV - 1 - dd
            acc = acc + cw_ref[k:k + 1, lo:hi] * sh[(dd - 1) * CHUNK:dd * CHUNK, :]
        return acc

    conv_xs = conv_part(xs_ref, pxs_ref, 0, d_ssd)
    conv_bc = conv_part(bc_ref, pbc_ref, d_ssd, d_ssd + 2 * gw)
    xs_a = _silu(conv_xs)
    bc_a = _silu(conv_bc)
    bm = bc_a[:, :gw]
    cm = bc_a[:, gw:]

    lane = lax.broadcasted_iota(jnp.int32, (CHUNK, LANE), 1)
    row = lax.broadcasted_iota(jnp.int32, (CHUNK, LANE), 0)
    head_lane = lane < n_heads
    dt_in = dt_ref[...] + dtb_ref[...]
    dt = jnp.maximum(dt_in, 0.0) + jnp.log1p(jnp.exp(-jnp.abs(dt_in)))
    dt = jnp.where(head_lane, dt, 0.0)
    a = dt * (-jnp.exp(alog_ref[...]))
    causal = row >= lane
    tri = jnp.where(causal, 1.0, 0.0).astype(BF16)
    a_cum = _dot_exact_rhs(tri, a)
    a_cum_t = a_cum.T
    ea = jnp.exp(a_cum)
    dec = jnp.exp(a_cum[CHUNK - 1:CHUNK, :] - a_cum)

    erow = lax.broadcasted_iota(jnp.int32, (LANE, d_ssd), 0)
    ecol = lax.broadcasted_iota(jnp.int32, (LANE, d_ssd), 1)
    expand = jnp.where(ecol // SSD_HEAD_DIM == erow, 1.0, 0.0).astype(BF16)
    dt_x = _dot_hi_mid_lhs(dt, expand)
    dtdec_x = _dot_hi_mid_lhs(dt * dec, expand)
    ea_x = _dot_exact_lhs(ea, expand)
    xdt_b = (xs_a * dt_x).astype(BF16)
    xsd_b = (xs_a * dtdec_x).astype(BF16)
    dskip = dskip_ref[...]

    lo_half = lane < SSD_HEAD_DIM
    pairs = hpg // 2
    for g in range(SSD_GROUPS):
        bg = bm[:, g * SSD_STATE:(g + 1) * SSD_STATE].astype(BF16)
        cg = cm[:, g * SSD_STATE:(g + 1) * SSD_STATE].astype(BF16)
        cb = lax.dot_general(cg, bg, (((1,), (1,)), ((), ())), preferred_element_type=F32)
        gsl = slice(g * gch, (g + 1) * gch)
        prev_t = state_ref[g]
        y_off = _dot(cg, prev_t.astype(BF16)) * ea_x[:, gsl]
        states_t = lax.dot_general(bg, xsd_b[:, gsl], (((0,), (0,)), ((), ())),
                                   preferred_element_type=F32)
        state_ref[g] = prev_t * ea_x[CHUNK - 1:CHUNK, gsl] + states_t
        for p in range(pairs):
            t = g * pairs + p
            tsl = slice(t * LANE, (t + 1) * LANE)
            xt = xdt_b[:, tsl]
            yd = None
            for q in range(2):
                h = 2 * t + q
                seg = a_cum[:, h:h + 1] - a_cum_t[h:h + 1, :]
                lmat = jnp.exp(jnp.where(causal, seg, -jnp.inf))
                mh = (cb * lmat).astype(BF16)
                xh = jnp.where(lo_half if q == 0 else jnp.logical_not(lo_half), xt, jnp.zeros_like(xt))
                part = _dot(mh, xh)
                yd = part if yd is None else yd + part
            y_ref[:, tsl] = yd + y_off[:, p * LANE:(p + 1) * LANE] + xs_a[:, tsl] * dskip[:, tsl]

    yz = y_ref[...] * _silu(z_ref[...].astype(F32))
    ms = jnp.mean(yz * yz, axis=-1, keepdims=True)
    o_ref[:, :d_ssd] = (yz * lax.rsqrt(ms + EPS) * nw_ref[...]).astype(BF16)


def _ret_body(q_ref, k_ref, v_ref, g_ref, cos_ref, sin_ref, gw_ref, gb_ref, o_ref, state_ref, qd_ref, kd_ref,
              *, n_heads, log_g, col0):
    dh = RET_HEAD_DIM
    cosf = cos_ref[...]
    sinf = sin_ref[...]
    row = lax.broadcasted_iota(jnp.int32, (CHUNK, CHUNK), 0)
    col = lax.broadcasted_iota(jnp.int32, (CHUNK, CHUNK), 1)
    causal = row >= col
    for h in range(n_heads):
        hs = slice(h * dh, (h + 1) * dh)
        qh = q_ref[:, hs].astype(F32)
        kh = k_ref[:, hs].astype(F32)
        qt = ((qh * cosf + pltpu.roll(qh, dh // 2, 1) * sinf) * qd_ref[h]).astype(BF16)
        kt = ((kh * cosf + pltpu.roll(kh, dh // 2, 1) * sinf) * kd_ref[h]).astype(BF16)
        vh = v_ref[:, hs]
        scores = lax.dot_general(qt, kt, (((1,), (1,)), ((), ())), preferred_element_type=F32)
        inner = _dot(jnp.where(causal, scores, 0.0).astype(BF16), vh)
        prev = state_ref[h]
        cross = _dot(qt, prev.astype(BF16))
        kv = lax.dot_general(kt, vh, (((0,), (0,)), ((), ())), preferred_element_type=F32)
        state_ref[h] = (prev + kv) * math.exp(CHUNK * log_g[h])
        o = inner + cross
        mu = jnp.mean(o, axis=-1, keepdims=True)
        oc = o - mu
        var = jnp.mean(oc * oc, axis=-1, keepdims=True)
        on = oc * lax.rsqrt(var + EPS) * gw_ref[:, hs] + gb_ref[:, hs]
        o_ref[:, col0 + h * dh:col0 + (h + 1) * dh] = (_silu(g_ref[:, hs].astype(F32)) * on).astype(BF16)


def _mixer_kernel(z_ref, xs_ref, bc_ref, pxs_ref, pbc_ref, dt_ref, q_ref, k_ref, v_ref, g_ref, cos_ref, sin_ref,
                  cw_ref, cb_ref, dtb_ref, alog_ref, dskip_ref, nw_ref, gw_ref, gb_ref,
                  o_ref, sstate_ref, y_ref, rstate_ref, qd_ref, kd_ref,
                  *, n_heads, d_ssd, ret_heads, log_g):
    b = pl.program_id(0)
    c = pl.program_id(1)

    @pl.when((b == 0) & (c == 0))
    def _():
        pos = lax.broadcasted_iota(jnp.int32, (CHUNK, RET_HEAD_DIM), 0).astype(F32) + 1.0
        for h in range(ret_heads):
            qd_ref[h] = jnp.exp(pos * log_g[h])
            kd_ref[h] = jnp.exp(pos * (-log_g[h])) * (RET_HEAD_DIM ** -0.5)

    @pl.when(c == 0)
    def _():
        sstate_ref[...] = jnp.zeros(sstate_ref.shape, F32)
        rstate_ref[...] = jnp.zeros(rstate_ref.shape, F32)

    _ssd_body(z_ref, xs_ref, bc_ref, pxs_ref, pbc_ref, dt_ref, cw_ref, cb_ref, dtb_ref, alog_ref, dskip_ref, nw_ref,
              o_ref, sstate_ref, y_ref, n_heads=n_heads, d_ssd=d_ssd, first_chunk=c == 0)
    _ret_body(q_ref, k_ref, v_ref, g_ref, cos_ref, sin_ref, gw_ref, gb_ref, o_ref, rstate_ref, qd_ref, kd_ref,
              n_heads=ret_heads, log_g=log_g, col0=d_ssd)


def _mixer_call(proj, dt_raw, cos_t, sin_t, conv_w, conv_b, dt_bias_p, a_log_p, d_skip_x, ssd_norm_w, gn_w, gn_b,
                *, layer, batch, seq, d_ssd, n_heads, d_ret, ret_heads):
    nc = seq // CHUNK
    l = layer
    depth = conv_w.shape[0]
    xbc_w = conv_w.shape[2]
    bc_w = xbc_w - d_ssd
    bc_blk = (2 * d_ssd + 4 * d_ret) // bc_w
    hpg = n_heads // SSD_GROUPS
    assert d_ssd == d_ret
    log_g = tuple(math.log(1.0 - 2.0 ** (-5.0 - h)) for h in range(ret_heads))
    tok = lambda blk: (lambda b, c: (b * nc + c, blk))
    before = lambda blk: (lambda b, c: (b * nc + jnp.maximum(c - 1, 0), blk))
    par = lambda b, c: (l, 0, 0)
    return pl.pallas_call(
        functools.partial(_mixer_kernel, n_heads=n_heads, d_ssd=d_ssd, ret_heads=ret_heads, log_g=log_g),
        grid=(batch, nc),
        in_specs=[
            pl.BlockSpec((CHUNK, d_ssd), tok(0)),
            pl.BlockSpec((CHUNK, d_ssd), tok(1)),
            pl.BlockSpec((CHUNK, bc_w), tok(bc_blk)),
            pl.BlockSpec((CHUNK, d_ssd), before(1)),
            pl.BlockSpec((CHUNK, bc_w), before(bc_blk)),
            pl.BlockSpec((CHUNK, LANE), tok(0)),
            pl.BlockSpec((CHUNK, d_ret), tok(2)),
            pl.BlockSpec((CHUNK, d_ret), tok(3)),
            pl.BlockSpec((CHUNK, d_ret), tok(4)),
            pl.BlockSpec((CHUNK, d_ret), tok(5)),
            pl.BlockSpec((CHUNK, RET_HEAD_DIM), lambda b, c: (c, 0)),
            pl.BlockSpec((CHUNK, RET_HEAD_DIM), lambda b, c: (c, 0)),
            pl.BlockSpec((None, SSD_CONV, xbc_w), par),
            pl.BlockSpec((None, 1, xbc_w), par),
            pl.BlockSpec((None, 1, LANE), par),
            pl.BlockSpec((None, 1, LANE), par),
            pl.BlockSpec((None, 1, d_ssd), par),
            pl.BlockSpec((None, 1, d_ssd), par),
            pl.BlockSpec((None, 1, d_ret), par),
            pl.BlockSpec((None, 1, d_ret), par),
        ],
        out_specs=pl.BlockSpec((CHUNK, d_ssd + d_ret), lambda b, c: (b * nc + c, 0)),
        out_shape=jax.ShapeDtypeStruct((batch * seq, d_ssd + d_ret), BF16),
        scratch_shapes=[
            pltpu.VMEM((SSD_GROUPS, SSD_STATE, hpg * SSD_HEAD_DIM), F32),
            pltpu.VMEM((CHUNK, d_ssd), F32),
            pltpu.VMEM((ret_heads, RET_HEAD_DIM, RET_HEAD_DIM), F32),
            pltpu.VMEM((ret_heads, CHUNK, RET_HEAD_DIM), F32),
            pltpu.VMEM((ret_heads, CHUNK, RET_HEAD_DIM), F32),
        ],
        compiler_params=pltpu.CompilerParams(
            dimension_semantics=("arbitrary", "arbitrary"), vmem_limit_bytes=VMEM_LIMIT),
        name="mixer",
    )(proj, proj, proj, proj, proj, dt_raw, proj, proj, proj, proj, cos_t, sin_t,
      conv_w, conv_b.reshape(depth, 1, xbc_w), dt_bias_p, a_log_p, d_skip_x,
      ssd_norm_w.reshape(depth, 1, d_ssd), gn_w.reshape(depth, 1, d_ret), gn_b.reshape(depth, 1, d_ret))


def _outproj_kernel(x_ref, mod_ref, y_ref, w_ref, o_ref, wb_ref, *, tn):
    @pl.when(pl.program_id(0) == 0)
    def _():
        def cast(r, carry):
            rows = pl.ds(pl.multiple_of(r * NORM_ROWS, NORM_ROWS), NORM_ROWS)
            wb_ref[rows, :] = w_ref[rows, :].astype(BF16)
            return carry

        lax.fori_loop(0, w_ref.shape[0] // NORM_ROWS, cast, 0)

    y = y_ref[...]
    for n0 in range(0, o_ref.shape[1], tn):
        cols = slice(n0, n0 + tn)
        o_ref[:, cols] = x_ref[:, cols] + mod_ref[5:6, cols] * _dot(y, wb_ref[:, cols])


def _outproj_call(x2, mods4, y_mix, w_out, *, layer, tiles_per_batch, tm):
    m, d = x2.shape
    d_mix = y_mix.shape[1]
    tr = tm // 2
    tpb = tiles_per_batch * 2
    l = layer
    return pl.pallas_call(
        functools.partial(_outproj_kernel, tn=512),
        grid=(m // tr,),
        in_specs=[
            pl.BlockSpec((tr, d), lambda i: (i, 0)),
            pl.BlockSpec((None, None, N_MOD, d), lambda i: (l, i // tpb, 0, 0)),
            pl.BlockSpec((tr, d_mix), lambda i: (i, 0)),
            pl.BlockSpec((None, d_mix, d), lambda i: (l, 0, 0), pipeline_mode=pl.Buffered(1)),
        ],
        out_specs=pl.BlockSpec((tr, d), lambda i: (i, 0)),
        out_shape=jax.ShapeDtypeStruct((m, d), F32),
        scratch_shapes=[pltpu.VMEM((d_mix, d), BF16)],
        compiler_params=pltpu.CompilerParams(
            dimension_semantics=("arbitrary",), vmem_limit_bytes=VMEM_LIMIT),
        name="outproj",
    )(x2, mods4, y_mix, w_out)


def _pad_lanes(v):
    return jnp.pad(v, ((0, 0), (0, LANE - v.shape[1])))[:, None, :]


def kernel(x, c, norm_w, mod_w, mod_b, ffn_w13, ffn_w2, w_in, conv_w, conv_b, dt_bias, a_log, d_skip,
           ssd_norm_w, ret_norm_w, ret_norm_b, w_out, final_norm_w):
    batch, seq, d = x.shape
    depth = norm_w.shape[0]
    n_heads = dt_bias.shape[1]
    d_ssd = ssd_norm_w.shape[1]
    d_ret = ret_norm_w.shape[1]
    ret_heads = d_ret // RET_HEAD_DIM
    xbc_w = conv_w.shape[2]
    m = batch * seq
    tm = min(ROW_TILE, seq)
    tiles_per_batch = seq // tm

    pad_rows = 16
    c_pad = jnp.pad(c, ((0, pad_rows - batch), (0, 0)))
    mods = _mods_call(c_pad, mod_w, mod_b)[:, :batch].reshape(depth, batch, N_MOD, d)

    w_in_t = jnp.swapaxes(w_in, 1, 2)

    half = RET_HEAD_DIM // 2
    inv = ROPE_BASE ** (-jnp.arange(0, RET_HEAD_DIM, 2, dtype=F32) / RET_HEAD_DIM)
    ang = jnp.arange(seq, dtype=F32)[:, None] * inv[None, :]
    cos_t = jnp.concatenate([jnp.cos(ang), jnp.cos(ang)], axis=-1)
    sin_t = jnp.concatenate([-jnp.sin(ang), jnp.sin(ang)], axis=-1)
    assert cos_t.shape == (seq, 2 * half)

    dt_bias_p = _pad_lanes(dt_bias)
    a_log_p = _pad_lanes(a_log)
    d_skip_x = jnp.repeat(d_skip, SSD_HEAD_DIM, axis=1)[:, None, :]
    final_w = final_norm_w.reshape(1, d)

    x2 = x.reshape(m, d)
    common = dict(tiles_per_batch=tiles_per_batch, tm=tm)
    for l in range(depth):
        x2, w_t = _ffn_call(x2, mods, norm_w, final_w, ffn_w13, ffn_w2, layer=l, half=0, sub=0, k0=0,
                            final=False, cast_src=w_in_t, **common)
        proj, dt_raw = _inproj_call(x2, mods, norm_w, w_t, layer=l, d_ssd=d_ssd, xbc_w=xbc_w, n_heads=n_heads,
                                    d_ret=d_ret, **common)
        y_mix = _mixer_call(proj, dt_raw, cos_t, sin_t, conv_w, conv_b, dt_bias_p, a_log_p, d_skip_x, ssd_norm_w,
                            ret_norm_w, ret_norm_b, layer=l, batch=batch, seq=seq, d_ssd=d_ssd,
                            n_heads=n_heads, d_ret=d_ret, ret_heads=ret_heads)
        x2 = _outproj_call(x2, mods, y_mix, w_out, layer=l, **common)
        x2 = _ffn_call(x2, mods, norm_w, final_w, ffn_w13, ffn_w2, layer=l, half=1, sub=2, k0=6,
                       final=(l == depth - 1), **common)
    return x2.reshape(batch, seq, d)
```

```python
import functools
import math

import jax
import jax.numpy as jnp
from jax import lax
from jax.experimental import pallas as pl
from jax.experimental.pallas import tpu as pltpu

F32 = jnp.float32
BF16 = jnp.bfloat16
EPS = 1e-6
LANE = 128
CHUNK = 128
SSD_HEAD_DIM = 64
SSD_GROUPS = 2
SSD_STATE = 128
SSD_CONV = 4
RET_HEAD_DIM = 128
ROPE_BASE = 10000.0
N_MOD = 9
VMEM_LIMIT = 56 * 1024 * 1024
ROW_TILE = 1024
NORM_ROWS = 32
CAST_ROWS = 48


def _silu(x):
    return x * jax.nn.sigmoid(x)


def _split3(x):
    hi = x.astype(BF16)
    r1 = x - hi.astype(F32)
    mid = r1.astype(BF16)
    lo = (r1 - mid.astype(F32)).astype(BF16)
    return hi, mid, lo


def _dot(a, b):
    return jnp.dot(a, b, preferred_element_type=F32)


def _dot_nt(a, b_t):
    return lax.dot_general(a, b_t, (((1,), (1,)), ((), ())), preferred_element_type=F32)


def _dot_exact_rhs(m_bf16, x):
    hi, mid, lo = _split3(x)
    return _dot(m_bf16, hi) + _dot(m_bf16, mid) + _dot(m_bf16, lo)


def _dot_exact_lhs(x, m_bf16):
    hi, mid, lo = _split3(x)
    return _dot(hi, m_bf16) + _dot(mid, m_bf16) + _dot(lo, m_bf16)


def _dot_hi_mid_lhs(x, m_bf16):
    hi = x.astype(BF16)
    mid = (x - hi.astype(F32)).astype(BF16)
    return _dot(hi, m_bf16) + _dot(mid, m_bf16)


def _norm_modulate_rows(x_ref, h_ref, mult, shift, rows_total, copy_ref=None):
    def body(r, carry):
        rows = pl.ds(pl.multiple_of(r * NORM_ROWS, NORM_ROWS), NORM_ROWS)
        xv = x_ref[rows, :]
        ms = jnp.mean(xv * xv, axis=-1, keepdims=True)
        h_ref[rows, :] = (x_ref[rows, :] * lax.rsqrt(ms + EPS) * mult + shift).astype(BF16)
        if copy_ref is not None:
            copy_ref[rows, :] = xv
        return carry

    lax.fori_loop(0, rows_total // NORM_ROWS, body, 0, unroll=4)


def _mods_kernel(c_ref, w_ref, b_ref, o_ref):
    ca = _silu(c_ref[...]).astype(BF16)
    o_ref[...] = _dot(ca, w_ref[...].astype(BF16)) + b_ref[...]


def _mods_call(c_pad, mod_w, mod_b, layers):
    depth, d, n = mod_w.shape
    rows = c_pad.shape[0]
    tn = 2048
    return pl.pallas_call(
        _mods_kernel,
        grid=(layers, n // tn),
        in_specs=[
            pl.BlockSpec((rows, d), lambda l, j: (0, 0)),
            pl.BlockSpec((None, d, tn), lambda l, j: (l, 0, j)),
            pl.BlockSpec((None, 1, tn), lambda l, j: (l, 0, j)),
        ],
        out_specs=pl.BlockSpec((None, rows, tn), lambda l, j: (l, 0, j)),
        out_shape=jax.ShapeDtypeStruct((layers, rows, n), F32),
        compiler_params=pltpu.CompilerParams(
            dimension_semantics=("arbitrary", "arbitrary"), vmem_limit_bytes=VMEM_LIMIT),
        name="mods",
    )(c_pad, mod_w, mod_b.reshape(depth, 1, n))


def _ffn_kernel(x_ref, mod_ref, nw_ref, fw_ref, wa_ref, wb0_ref, wb1_ref, w2_ref, *rest,
                sub, k0, nj, has_tail, final, tm, d, side_cast, side_mods, batch):
    j = pl.program_id(1)
    if side_cast:
        src_ref, o_ref, dst_ref, h_ref, wcat_ref, w2c_ref = rest
    elif side_mods:
        c_ref, mw_ref, mb_ref, o_ref, mo_ref, h_ref, wcat_ref, w2c_ref, cb_ref = rest

        @pl.when((pl.program_id(0) == 0) & (j == 0))
        def _():
            ct = _silu(c_ref[...]).T
            for b in range(batch):
                cb_ref[b] = jnp.broadcast_to(ct[:, b:b + 1], (d, LANE))
    else:
        o_ref, h_ref, wcat_ref, w2c_ref = rest

    @pl.when(j == 0)
    def _():
        shift = mod_ref[k0:k0 + 1, :]
        scale = mod_ref[k0 + 1:k0 + 2, :]
        mult = nw_ref[sub:sub + 1, :] * (1.0 + scale)
        _norm_modulate_rows(x_ref, h_ref, mult, shift, tm, copy_ref=o_ref)

    if side_cast:
        dst_ref[...] = src_ref[...].astype(BF16)
    if side_mods:
        w = mw_ref[...]
        brow = lax.broadcasted_iota(jnp.int32, mo_ref.shape, 0)
        acc = jnp.zeros(mo_ref.shape, F32)
        for b in range(batch):
            acc = jnp.where(brow == b, jnp.sum(w * cb_ref[b], axis=0, keepdims=True), acc)
        mo_ref[...] = acc + mb_ref[...]

    wcat_ref[:, 0 * LANE:2 * LANE] = wa_ref[...].astype(BF16)
    wcat_ref[:, 2 * LANE:3 * LANE] = wb0_ref[...].astype(BF16)
    wcat_ref[:, 3 * LANE:4 * LANE] = wb1_ref[...].astype(BF16)
    ab = _dot(h_ref[...], wcat_ref[...])
    a = ab[:, :2 * LANE]
    b = ab[:, 2 * LANE:]
    g = _silu(a) * b
    w2_hi = w2_ref[LANE:2 * LANE, :]
    if has_tail:
        live = j < nj - 1
        col = lax.broadcasted_iota(jnp.int32, g.shape, 1)
        g = jnp.where((col < LANE) | live, g, 0.0)
        w2_hi = jnp.where(live, w2_hi, 0.0)
    gb = g.astype(BF16)
    w2c_ref[0:LANE, :] = w2_ref[0:LANE, :].astype(BF16)
    w2c_ref[LANE:2 * LANE, :] = w2_hi.astype(BF16)
    nc = 512
    for n0 in range(0, d, nc):
        half_gate = 0.5 * mod_ref[k0 + 2:k0 + 3, n0:n0 + nc]
        o_ref[:, n0:n0 + nc] += half_gate * _dot(gb, w2c_ref[:, n0:n0 + nc])

    if final:
        @pl.when(j == nj - 1)
        def _():
            fw = fw_ref[...]

            def fin(r, carry):
                rows = pl.ds(pl.multiple_of(r * NORM_ROWS, NORM_ROWS), NORM_ROWS)
                xn = o_ref[rows, :]
                ms = jnp.mean(xn * xn, axis=-1, keepdims=True)
                o_ref[rows, :] = xn * lax.rsqrt(ms + EPS) * fw
                return carry

            lax.fori_loop(0, tm // NORM_ROWS, fin, 0, unroll=2)


def _ffn_call(x2, mods3, norm_w, final_w, w13, w2, *, layer, half, sub, k0, final, tiles_per_batch, tm,
              cast_src=None, next_mods=None):
    m, d = x2.shape
    d_ff = w2.shape[2]
    nblk = d_ff // LANE
    nj = (nblk + 1) // 2
    has_tail = nblk % 2 == 1
    last = nblk - 1
    l, s = layer, half
    batch = mods3.shape[0]
    kern = functools.partial(_ffn_kernel, sub=sub, k0=k0, nj=nj, has_tail=has_tail, final=final, tm=tm, d=d,
                             side_cast=cast_src is not None, side_mods=next_mods is not None, batch=batch)
    extra_in, extra_out_spec, extra_out_shape, extra_args, extra_scratch = [], [], [], [], []
    assert cast_src is None or next_mods is None
    if cast_src is not None and (cast_src.shape[1] // CAST_ROWS) > (m // tm) * nj:
        return _ffn_call(x2, mods3, norm_w, final_w, w13, w2, layer=layer, half=half, sub=sub, k0=k0, final=final,
                         tiles_per_batch=tiles_per_batch, tm=tm), cast_src[layer].astype(BF16)
    if next_mods is not None:
        c8, mod_w, mod_b = next_mods
        n = mod_w.shape[2]
        nmb = n // LANE
        assert nmb <= (m // tm) * nj and c8.shape[0] == 8
        mblk = lambda i, j: jnp.minimum(i * nj + j, nmb - 1)
        extra_in = [
            pl.BlockSpec((8, d), lambda i, j: (0, 0)),
            pl.BlockSpec((None, d, LANE), lambda i, j: (l + 1, 0, mblk(i, j))),
            pl.BlockSpec((None, 1, LANE), lambda i, j: (l + 1, 0, mblk(i, j))),
        ]
        extra_out_spec = [pl.BlockSpec((8, LANE), lambda i, j: (0, mblk(i, j)))]
        extra_out_shape = [jax.ShapeDtypeStruct((8, n), F32)]
        extra_args = [c8, mod_w, mod_b.reshape(mod_b.shape[0], 1, n)]
        extra_scratch = [pltpu.VMEM((batch, d, LANE), F32)]
    if cast_src is not None:
        rows = cast_src.shape[1]
        n_cast = rows // CAST_ROWS
        assert rows % CAST_ROWS == 0 and cast_src.shape[2] == d
        cast_blk = lambda i, j: jnp.minimum(i * nj + j, n_cast - 1)
        extra_in = [pl.BlockSpec((None, CAST_ROWS, d), lambda i, j: (l, cast_blk(i, j), 0))]
        extra_out_spec = [pl.BlockSpec((CAST_ROWS, d), lambda i, j: (cast_blk(i, j), 0))]
        extra_out_shape = [jax.ShapeDtypeStruct((rows, d), BF16)]
        extra_args = [cast_src]
    out = pl.pallas_call(
        kern,
        grid=(m // tm, nj),
        in_specs=[
            pl.BlockSpec((tm, d), lambda i, j: (i, 0)),
            pl.BlockSpec((None, N_MOD, d), lambda i, j: (i // tiles_per_batch, 0, 0)),
            pl.BlockSpec((None, 3, d), lambda i, j: (l, 0, 0)),
            pl.BlockSpec((1, d), lambda i, j: (0, 0)),
            pl.BlockSpec((None, None, d, 2 * LANE), lambda i, j: (l, s, 0, j)),
            pl.BlockSpec((None, None, d, LANE), lambda i, j: (l, s, 0, nblk + 2 * j)),
            pl.BlockSpec((None, None, d, LANE), lambda i, j: (l, s, 0, nblk + jnp.minimum(2 * j + 1, last))),
            pl.BlockSpec((None, None, 2 * LANE, d), lambda i, j: (l, s, j, 0)),
        ] + extra_in,
        out_specs=[pl.BlockSpec((tm, d), lambda i, j: (i, 0))] + extra_out_spec,
        out_shape=[jax.ShapeDtypeStruct((m, d), F32)] + extra_out_shape,
        scratch_shapes=[
            pltpu.VMEM((tm, d), BF16),
            pltpu.VMEM((d, 4 * LANE), BF16),
            pltpu.VMEM((2 * LANE, d), BF16),
        ] + extra_scratch,
        compiler_params=pltpu.CompilerParams(
            dimension_semantics=("arbitrary", "arbitrary"), vmem_limit_bytes=VMEM_LIMIT),
        name=f"ffn_{sub}",
    )(x2, mods3, norm_w, final_w, w13, w13, w13, w2, *extra_args)
    return out if extra_args else out[0]


def _inproj_kernel(x_ref, mod_ref, nw_ref, w_ref, o_ref, dt_ref, h_ref, *, tm, tn, segments, dt_row):
    shift = mod_ref[3:4, :]
    scale = mod_ref[4:5, :]
    mult = nw_ref[1:2, :] * (1.0 + scale)
    _norm_modulate_rows(x_ref, h_ref, mult, shift, tm)
    h = h_ref[...]
    dt_ref[...] = _dot_nt(h, w_ref[dt_row:dt_row + LANE, :])
    for col0, row0, width in segments:
        for n0 in range(0, width, tn):
            o_ref[:, col0 + n0:col0 + n0 + tn] = _dot_nt(h, w_ref[row0 + n0:row0 + n0 + tn, :]).astype(BF16)


def _inproj_call(x2, mods3, norm_w, w_t, *, layer, tiles_per_batch, tm, d_ssd, xbc_w, n_heads, d_ret):
    m, d = x2.shape
    n_in = w_t.shape[0]
    tr = tm // 2
    tpb = tiles_per_batch * 2
    l = layer
    o_dt = d_ssd + xbc_w
    o_q = o_dt + n_heads
    bc_w = xbc_w - d_ssd
    segments = ((0, 0, 2 * d_ssd), (2 * d_ssd, o_q, 4 * d_ret), (2 * d_ssd + 4 * d_ret, 2 * d_ssd, bc_w))
    n = 2 * d_ssd + 4 * d_ret + bc_w
    assert all(r % 16 == 0 and w % 512 == 0 for _, r, w in segments) and o_dt + LANE <= n_in
    return pl.pallas_call(
        functools.partial(_inproj_kernel, tm=tr, tn=512, segments=segments, dt_row=o_dt),
        grid=(m // tr,),
        in_specs=[
            pl.BlockSpec((tr, d), lambda i: (i, 0)),
            pl.BlockSpec((None, N_MOD, d), lambda i: (i // tpb, 0, 0)),
            pl.BlockSpec((None, 3, d), lambda i: (l, 0, 0)),
            pl.BlockSpec((n_in, d), lambda i: (0, 0), pipeline_mode=pl.Buffered(1)),
        ],
        out_specs=[
            pl.BlockSpec((tr, n), lambda i: (i, 0)),
            pl.BlockSpec((tr, LANE), lambda i: (i, 0)),
        ],
        out_shape=[
            jax.ShapeDtypeStruct((m, n), BF16),
            jax.ShapeDtypeStruct((m, LANE), F32),
        ],
        scratch_shapes=[pltpu.VMEM((tr, d), BF16)],
        compiler_params=pltpu.CompilerParams(
            dimension_semantics=("arbitrary",), vmem_limit_bytes=VMEM_LIMIT),
        name="inproj",
    )(x2, mods3, norm_w, w_t)


def _ssd_body(z_ref, xs_ref, bc_ref, pxs_ref, pbc_ref, dt_ref, cw_ref, cb_ref, dtb_ref, alog_ref, dskip_ref, nw_ref,
              o_ref, state_ref, y_ref, *, n_heads, d_ssd, first_chunk):
    gw = SSD_GROUPS * SSD_STATE
    hpg = n_heads // SSD_GROUPS
    gch = hpg * SSD_HEAD_DIM

    srow = lax.broadcasted_iota(jnp.int32, ((SSD_CONV - 1) * CHUNK, 2 * CHUNK), 0)
    scol = lax.broadcasted_iota(jnp.int32, ((SSD_CONV - 1) * CHUNK, 2 * CHUNK), 1)
    src = (srow & (CHUNK - 1)) - (srow // CHUNK + 1) + CHUNK
    keep = (scol >= CHUNK) | jnp.logical_not(first_chunk)
    shift = jnp.where((scol == src) & keep, 1.0, 0.0).astype(BF16)

    def conv_part(cur_ref, prev_ref, lo, hi):
        cur = cur_ref[...]
        sh = _dot(shift, jnp.concatenate([prev_ref[...], cur], axis=0))
        acc = cb_ref[:, lo:hi] + cw_ref[SSD_CONV - 1:SSD_CONV, lo:hi] * cur.astype(F32)
        for dd in range(1, SSD_CONV):
            k = SSD_CONV - 1 - dd
            acc = acc + cw_ref[k:k + 1, lo:hi] * sh[(dd - 1) * CHUNK:dd * CHUNK, :]
        return acc

    conv_xs = conv_part(xs_ref, pxs_ref, 0, d_ssd)
    conv_bc = conv_part(bc_ref, pbc_ref, d_ssd, d_ssd + 2 * gw)
    xs_a = _silu(conv_xs)
    bc_a = _silu(conv_bc)
    bm = bc_a[:, :gw]
    cm = bc_a[:, gw:]

    lane = lax.broadcasted_iota(jnp.int32, (CHUNK, LANE), 1)
    row = lax.broadcasted_iota(jnp.int32, (CHUNK, LANE), 0)
    head_lane = lane < n_heads
    dt_in = dt_ref[...] + dtb_ref[...]
    dt = jnp.maximum(dt_in, 0.0) + jnp.log1p(jnp.exp(-jnp.abs(dt_in)))
    dt = jnp.where(head_lane, dt, 0.0)
    a = dt * (-jnp.exp(alog_ref[...]))
    causal = row >= lane
    tri = jnp.where(causal, 1.0, 0.0).astype(BF16)
    a_cum = _dot_exact_rhs(tri, a)
    a_cum_t = a_cum.T
    ea = jnp.exp(a_cum)
    dec = jnp.exp(a_cum[CHUNK - 1:CHUNK, :] - a_cum)

    erow = lax.broadcasted_iota(jnp.int32, (LANE, d_ssd), 0)
    ecol = lax.broadcasted_iota(jnp.int32, (LANE, d_ssd), 1)
    expand = jnp.where(ecol // SSD_HEAD_DIM == erow, 1.0, 0.0).astype(BF16)
    dt_x = _dot_hi_mid_lhs(dt, expand)
    dtdec_x = _dot_hi_mid_lhs(dt * dec, expand)
    ea_x = _dot_exact_lhs(ea, expand)
    xdt_b = (xs_a * dt_x).astype(BF16)
    xsd_b = (xs_a * dtdec_x).astype(BF16)
    dskip = dskip_ref[...]

    lo_half = lane < SSD_HEAD_DIM
    pairs = hpg // 2
    for g in range(SSD_GROUPS):
        bg = bm[:, g * SSD_STATE:(g + 1) * SSD_STATE].astype(BF16)
        cg = cm[:, g * SSD_STATE:(g + 1) * SSD_STATE].astype(BF16)
        cb = lax.dot_general(cg, bg, (((1,), (1,)), ((), ())), preferred_element_type=F32)
        gsl = slice(g * gch, (g + 1) * gch)
        prev_t = state_ref[g]
        y_off = _dot(cg, prev_t.astype(BF16)) * ea_x[:, gsl]
        states_t = lax.dot_general(bg, xsd_b[:, gsl], (((0,), (0,)), ((), ())),
                                   preferred_element_type=F32)
        state_ref[g] = prev_t * ea_x[CHUNK - 1:CHUNK, gsl] + states_t
        for p in range(pairs):
            t = g * pairs + p
            tsl = slice(t * LANE, (t + 1) * LANE)
            xt = xdt_b[:, tsl]
            yd = None
            for q in range(2):
                h = 2 * t + q
                seg = a_cum[:, h:h + 1] - a_cum_t[h:h + 1, :]
                lmat = jnp.exp(jnp.where(causal, seg, -jnp.inf))
                mh = (cb * lmat).astype(BF16)
                xh = jnp.where(lo_half if q == 0 else jnp.logical_not(lo_half), xt, jnp.zeros_like(xt))
                part = _dot(mh, xh)
                yd = part if yd is None else yd + part
            y_ref[:, tsl] = yd + y_off[:, p * LANE:(p + 1) * LANE] + xs_a[:, tsl] * dskip[:, tsl]

    yz = y_ref[...] * _silu(z_ref[...].astype(F32))
    ms = jnp.mean(yz * yz, axis=-1, keepdims=True)
    o_ref[:, :d_ssd] = (yz * lax.rsqrt(ms + EPS) * nw_ref[...]).astype(BF16)


def _ret_body(q_ref, k_ref, v_ref, g_ref, cos_ref, sin_ref, gw_ref, gb_ref, o_ref, state_ref, qd_ref, kd_ref,
              *, n_heads, log_g, col0):
    dh = RET_HEAD_DIM
    cosf = cos_ref[...]
    sinf = sin_ref[...]
    row = lax.broadcasted_iota(jnp.int32, (CHUNK, CHUNK), 0)
    col = lax.broadcasted_iota(jnp.int32, (CHUNK, CHUNK), 1)
    causal = row >= col
    for h in range(n_heads):
        hs = slice(h * dh, (h + 1) * dh)
        qh = q_ref[:, hs].astype(F32)
        kh = k_ref[:, hs].astype(F32)
        qt = ((qh * cosf + pltpu.roll(qh, dh // 2, 1) * sinf) * qd_ref[h]).astype(BF16)
        kt = ((kh * cosf + pltpu.roll(kh, dh // 2, 1) * sinf) * kd_ref[h]).astype(BF16)
        vh = v_ref[:, hs]
        scores = lax.dot_general(qt, kt, (((1,), (1,)), ((), ())), preferred_element_type=F32)
        inner = _dot(jnp.where(causal, scores, 0.0).astype(BF16), vh)
        prev = state_ref[h]
        cross = _dot(qt, prev.astype(BF16))
        kv = lax.dot_general(kt, vh, (((0,), (0,)), ((), ())), preferred_element_type=F32)
        state_ref[h] = (prev + kv) * math.exp(CHUNK * log_g[h])
        o = inner + cross
        mu = jnp.mean(o, axis=-1, keepdims=True)
        oc = o - mu
        var = jnp.mean(oc * oc, axis=-1, keepdims=True)
        on = oc * lax.rsqrt(var + EPS) * gw_ref[:, hs] + gb_ref[:, hs]
        o_ref[:, col0 + h * dh:col0 + (h + 1) * dh] = (_silu(g_ref[:, hs].astype(F32)) * on).astype(BF16)


def _mixer_kernel(z_ref, xs_ref, bc_ref, pxs_ref, pbc_ref, dt_ref, q_ref, k_ref, v_ref, g_ref, cos_ref, sin_ref,
                  cw_ref, cb_ref, dtb_ref, alog_ref, dskip_ref, nw_ref, gw_ref, gb_ref,
                  o_ref, sstate_ref, y_ref, rstate_ref, qd_ref, kd_ref,
                  *, n_heads, d_ssd, ret_heads, log_g):
    b = pl.program_id(0)
    c = pl.program_id(1)

    @pl.when((b == 0) & (c == 0))
    def _():
        pos = lax.broadcasted_iota(jnp.int32, (CHUNK, RET_HEAD_DIM), 0).astype(F32) + 1.0
        for h in range(ret_heads):
            qd_ref[h] = jnp.exp(pos * log_g[h])
            kd_ref[h] = jnp.exp(pos * (-log_g[h])) * (RET_HEAD_DIM ** -0.5)

    @pl.when(c == 0)
    def _():
        sstate_ref[...] = jnp.zeros(sstate_ref.shape, F32)
        rstate_ref[...] = jnp.zeros(rstate_ref.shape, F32)

    _ssd_body(z_ref, xs_ref, bc_ref, pxs_ref, pbc_ref, dt_ref, cw_ref, cb_ref, dtb_ref, alog_ref, dskip_ref, nw_ref,
              o_ref, sstate_ref, y_ref, n_heads=n_heads, d_ssd=d_ssd, first_chunk=c == 0)
    _ret_body(q_ref, k_ref, v_ref, g_ref, cos_ref, sin_ref, gw_ref, gb_ref, o_ref, rstate_ref, qd_ref, kd_ref,
              n_heads=ret_heads, log_g=log_g, col0=d_ssd)


def _mixer_call(proj, dt_raw, cos_t, sin_t, conv_w, conv_b, dt_bias_p, a_log_p, d_skip_x, ssd_norm_w, gn_w, gn_b,
                *, layer, batch, seq, d_ssd, n_heads, d_ret, ret_heads):
    nc = seq // CHUNK
    l = layer
    depth = conv_w.shape[0]
    xbc_w = conv_w.shape[2]
    bc_w = xbc_w - d_ssd
    bc_blk = (2 * d_ssd + 4 * d_ret) // bc_w
    hpg = n_heads // SSD_GROUPS
    assert d_ssd == d_ret
    log_g = tuple(math.log(1.0 - 2.0 ** (-5.0 - h)) for h in range(ret_heads))
    tok = lambda blk: (lambda b, c: (b * nc + c, blk))
    before = lambda blk: (lambda b, c: (b * nc + jnp.maximum(c - 1, 0), blk))
    par = lambda b, c: (l, 0, 0)
    return pl.pallas_call(
        functools.partial(_mixer_kernel, n_heads=n_heads, d_ssd=d_ssd, ret_heads=ret_heads, log_g=log_g),
        grid=(batch, nc),
        in_specs=[
            pl.BlockSpec((CHUNK, d_ssd), tok(0)),
            pl.BlockSpec((CHUNK, d_ssd), tok(1)),
            pl.BlockSpec((CHUNK, bc_w), tok(bc_blk)),
            pl.BlockSpec((CHUNK, d_ssd), before(1)),
            pl.BlockSpec((CHUNK, bc_w), before(bc_blk)),
            pl.BlockSpec((CHUNK, LANE), tok(0)),
            pl.BlockSpec((CHUNK, d_ret), tok(2)),
            pl.BlockSpec((CHUNK, d_ret), tok(3)),
            pl.BlockSpec((CHUNK, d_ret), tok(4)),
            pl.BlockSpec((CHUNK, d_ret), tok(5)),
            pl.BlockSpec((CHUNK, RET_HEAD_DIM), lambda b, c: (c, 0)),
            pl.BlockSpec((CHUNK, RET_HEAD_DIM), lambda b, c: (c, 0)),
            pl.BlockSpec((None, SSD_CONV, xbc_w), par),
            pl.BlockSpec((None, 1, xbc_w), par),
            pl.BlockSpec((None, 1, LANE), par),
            pl.BlockSpec((None, 1, LANE), par),
            pl.BlockSpec((None, 1, d_ssd), par),
            pl.BlockSpec((None, 1, d_ssd), par),
            pl.BlockSpec((None, 1, d_ret), par),
            pl.BlockSpec((None, 1, d_ret), par),
        ],
        out_specs=pl.BlockSpec((CHUNK, d_ssd + d_ret), lambda b, c: (b * nc + c, 0)),
        out_shape=jax.ShapeDtypeStruct((batch * seq, d_ssd + d_ret), BF16),
        scratch_shapes=[
            pltpu.VMEM((SSD_GROUPS, SSD_STATE, hpg * SSD_HEAD_DIM), F32),
            pltpu.VMEM((CHUNK, d_ssd), F32),
            pltpu.VMEM((ret_heads, RET_HEAD_DIM, RET_HEAD_DIM), F32),
            pltpu.VMEM((ret_heads, CHUNK, RET_HEAD_DIM), F32),
            pltpu.VMEM((ret_heads, CHUNK, RET_HEAD_DIM), F32),
        ],
        compiler_params=pltpu.CompilerParams(
            dimension_semantics=("arbitrary", "arbitrary"), vmem_limit_bytes=VMEM_LIMIT),
        name="mixer",
    )(proj, proj, proj, proj, proj, dt_raw, proj, proj, proj, proj, cos_t, sin_t,
      conv_w, conv_b.reshape(depth, 1, xbc_w), dt_bias_p, a_log_p, d_skip_x,
      ssd_norm_w.reshape(depth, 1, d_ssd), gn_w.reshape(depth, 1, d_ret), gn_b.reshape(depth, 1, d_ret))


def _outproj_kernel(x_ref, mod_ref, y_ref, w_ref, o_ref, wb_ref, *, tn):
    @pl.when(pl.program_id(0) == 0)
    def _():
        def cast(r, carry):
            rows = pl.ds(pl.multiple_of(r * NORM_ROWS, NORM_ROWS), NORM_ROWS)
            wb_ref[rows, :] = w_ref[rows, :].astype(BF16)
            return carry

        lax.fori_loop(0, w_ref.shape[0] // NORM_ROWS, cast, 0)

    y = y_ref[...]
    for n0 in range(0, o_ref.shape[1], tn):
        cols = slice(n0, n0 + tn)
        o_ref[:, cols] = x_ref[:, cols] + mod_ref[5:6, cols] * _dot(y, wb_ref[:, cols])


def _outproj_call(x2, mods3, y_mix, w_out, *, layer, tiles_per_batch, tm):
    m, d = x2.shape
    d_mix = y_mix.shape[1]
    tr = tm // 2
    tpb = tiles_per_batch * 2
    l = layer
    return pl.pallas_call(
        functools.partial(_outproj_kernel, tn=512),
        grid=(m // tr,),
        in_specs=[
            pl.BlockSpec((tr, d), lambda i: (i, 0)),
            pl.BlockSpec((None, N_MOD, d), lambda i: (i // tpb, 0, 0)),
            pl.BlockSpec((tr, d_mix), lambda i: (i, 0)),
            pl.BlockSpec((None, d_mix, d), lambda i: (l, 0, 0), pipeline_mode=pl.Buffered(1)),
        ],
        out_specs=pl.BlockSpec((tr, d), lambda i: (i, 0)),
        out_shape=jax.ShapeDtypeStruct((m, d), F32),
        scratch_shapes=[pltpu.VMEM((d_mix, d), BF16)],
        compiler_params=pltpu.CompilerParams(
            dimension_semantics=("arbitrary",), vmem_limit_bytes=VMEM_LIMIT),
        name="outproj",
    )(x2, mods3, y_mix, w_out)


def _pad_lanes(v):
    return jnp.pad(v, ((0, 0), (0, LANE - v.shape[1])))[:, None, :]


def kernel(x, c, norm_w, mod_w, mod_b, ffn_w13, ffn_w2, w_in, conv_w, conv_b, dt_bias, a_log, d_skip,
           ssd_norm_w, ret_norm_w, ret_norm_b, w_out, final_norm_w):
    batch, seq, d = x.shape
    depth = norm_w.shape[0]
    n_heads = dt_bias.shape[1]
    d_ssd = ssd_norm_w.shape[1]
    d_ret = ret_norm_w.shape[1]
    ret_heads = d_ret // RET_HEAD_DIM
    xbc_w = conv_w.shape[2]
    m = batch * seq
    tm = min(ROW_TILE, seq)
    tiles_per_batch = seq // tm

    pad_rows = 16
    c_pad = jnp.pad(c, ((0, pad_rows - batch), (0, 0)))
    ffn_steps = (m // tm) * ((ffn_w2.shape[2] // LANE + 1) // 2)
    hosted = (N_MOD * d) // LANE <= ffn_steps
    mods_all = _mods_call(c_pad, mod_w, mod_b, 1 if hosted else depth)[:, :batch].reshape(-1, batch, N_MOD, d)
    mods = mods_all[0]
    next_mods = (c_pad[:8], mod_w, mod_b)

    w_in_t = jnp.swapaxes(w_in, 1, 2)

    half = RET_HEAD_DIM // 2
    inv = ROPE_BASE ** (-jnp.arange(0, RET_HEAD_DIM, 2, dtype=F32) / RET_HEAD_DIM)
    ang = jnp.arange(seq, dtype=F32)[:, None] * inv[None, :]
    cos_t = jnp.concatenate([jnp.cos(ang), jnp.cos(ang)], axis=-1)
    sin_t = jnp.concatenate([-jnp.sin(ang), jnp.sin(ang)], axis=-1)
    assert cos_t.shape == (seq, 2 * half)

    dt_bias_p = _pad_lanes(dt_bias)
    a_log_p = _pad_lanes(a_log)
    d_skip_x = jnp.repeat(d_skip, SSD_HEAD_DIM, axis=1)[:, None, :]
    final_w = final_norm_w.reshape(1, d)

    x2 = x.reshape(m, d)
    common = dict(tiles_per_batch=tiles_per_batch, tm=tm)
    for l in range(depth):
        x2, w_t = _ffn_call(x2, mods, norm_w, final_w, ffn_w13, ffn_w2, layer=l, half=0, sub=0, k0=0,
                            final=False, cast_src=w_in_t, **common)
        proj, dt_raw = _inproj_call(x2, mods, norm_w, w_t, layer=l, d_ssd=d_ssd, xbc_w=xbc_w, n_heads=n_heads,
                                    d_ret=d_ret, **common)
        y_mix = _mixer_call(proj, dt_raw, cos_t, sin_t, conv_w, conv_b, dt_bias_p, a_log_p, d_skip_x, ssd_norm_w,
                            ret_norm_w, ret_norm_b, layer=l, batch=batch, seq=seq, d_ssd=d_ssd,
                            n_heads=n_heads, d_ret=d_ret, ret_heads=ret_heads)
        x2 = _outproj_call(x2, mods, y_mix, w_out, layer=l, **common)
        if l + 1 < depth and not hosted:
            x2 = _ffn_call(x2, mods, norm_w, final_w, ffn_w13, ffn_w2, layer=l, half=1, sub=2, k0=6,
                           final=False, **common)
            mods = mods_all[l + 1]
        elif l + 1 < depth:
            x2, mods_up = _ffn_call(x2, mods, norm_w, final_w, ffn_w13, ffn_w2, layer=l, half=1, sub=2, k0=6,
                                    final=False, next_mods=next_mods, **common)
            mods = mods_up[:batch].reshape(batch, N_MOD, d)
        else:
            x2 = _ffn_call(x2, mods, norm_w, final_w, ffn_w13, ffn_w2, layer=l, half=1, sub=2, k0=6,
                           final=True, **common)
    return x2.reshape(batch, seq, d)
```

```python
import functools
import math

import jax
import jax.numpy as jnp
from jax import lax
from jax.experimental import pallas as pl
from jax.experimental.pallas import tpu as pltpu

F32 = jnp.float32
BF16 = jnp.bfloat16
EPS = 1e-6
LANE = 128
CHUNK = 128
SSD_HEAD_DIM = 64
SSD_GROUPS = 2
SSD_STATE = 128
SSD_CONV = 4
RET_HEAD_DIM = 128
ROPE_BASE = 10000.0
N_MOD = 9
VMEM_LIMIT = 56 * 1024 * 1024
ROW_TILE = 1024
NORM_ROWS = 32
CAST_ROWS = 48


def _silu(x):
    return x * jax.nn.sigmoid(x)


def _split3(x):
    hi = x.astype(BF16)
    r1 = x - hi.astype(F32)
    mid = r1.astype(BF16)
    lo = (r1 - mid.astype(F32)).astype(BF16)
    return hi, mid, lo


def _dot(a, b):
    return jnp.dot(a, b, preferred_element_type=F32)


def _dot_nt(a, b_t):
    return lax.dot_general(a, b_t, (((1,), (1,)), ((), ())), preferred_element_type=F32)


def _dot_exact_rhs(m_bf16, x):
    hi, mid, lo = _split3(x)
    return _dot(m_bf16, hi) + _dot(m_bf16, mid) + _dot(m_bf16, lo)


def _dot_exact_lhs(x, m_bf16):
    hi, mid, lo = _split3(x)
    return _dot(hi, m_bf16) + _dot(mid, m_bf16) + _dot(lo, m_bf16)


def _dot_hi_mid_lhs(x, m_bf16):
    hi = x.astype(BF16)
    mid = (x - hi.astype(F32)).astype(BF16)
    return _dot(hi, m_bf16) + _dot(mid, m_bf16)


def _norm_modulate_rows(x_ref, h_ref, mult, shift, rows_total, copy_ref=None):
    def body(r, carry):
        rows = pl.ds(pl.multiple_of(r * NORM_ROWS, NORM_ROWS), NORM_ROWS)
        xv = x_ref[rows, :]
        ms = jnp.mean(xv * xv, axis=-1, keepdims=True)
        h_ref[rows, :] = (x_ref[rows, :] * lax.rsqrt(ms + EPS) * mult + shift).astype(BF16)
        if copy_ref is not None:
            copy_ref[rows, :] = xv
        return carry

    lax.fori_loop(0, rows_total // NORM_ROWS, body, 0, unroll=4)


def _mods_kernel(c_ref, w_ref, b_ref, o_ref):
    ca = _silu(c_ref[...]).astype(BF16)
    o_ref[...] = _dot(ca, w_ref[...].astype(BF16)) + b_ref[...]


def _mods_call(c_pad, mod_w, mod_b, layers):
    depth, d, n = mod_w.shape
    rows = c_pad.shape[0]
    tn = 2048
    return pl.pallas_call(
        _mods_kernel,
        grid=(layers, n // tn),
        in_specs=[
            pl.BlockSpec((rows, d), lambda l, j: (0, 0)),
            pl.BlockSpec((None, d, tn), lambda l, j: (l, 0, j)),
            pl.BlockSpec((None, 1, tn), lambda l, j: (l, 0, j)),
        ],
        out_specs=pl.BlockSpec((None, rows, tn), lambda l, j: (l, 0, j)),
        out_shape=jax.ShapeDtypeStruct((layers, rows, n), F32),
        compiler_params=pltpu.CompilerParams(
            dimension_semantics=("arbitrary", "arbitrary"), vmem_limit_bytes=VMEM_LIMIT),
        name="mods",
    )(c_pad, mod_w, mod_b.reshape(depth, 1, n))


def _ffn_kernel(x_ref, mod_ref, nw_ref, fw_ref, wa_ref, wb0_ref, wb1_ref, w2_ref, *rest,
                sub, k0, nj, has_tail, final, tm, d, side_cast, side_mods, batch):
    j = pl.program_id(1)
    if side_cast:
        src_ref, o_ref, dst_ref, h_ref, wcat_ref, w2c_ref = rest
    elif side_mods:
        c_ref, mw_ref, mb_ref, o_ref, mo_ref, h_ref, wcat_ref, w2c_ref, cb_ref = rest

        @pl.when((pl.program_id(0) == 0) & (j == 0))
        def _():
            ct = _silu(c_ref[...]).T
            for b in range(batch):
                cb_ref[b] = jnp.broadcast_to(ct[:, b:b + 1], (d, LANE))
    else:
        o_ref, h_ref, wcat_ref, w2c_ref = rest

    @pl.when(j == 0)
    def _():
        shift = mod_ref[k0:k0 + 1, :]
        scale = mod_ref[k0 + 1:k0 + 2, :]
        mult = nw_ref[sub:sub + 1, :] * (1.0 + scale)
        _norm_modulate_rows(x_ref, h_ref, mult, shift, tm, copy_ref=o_ref)

    if side_cast:
        dst_ref[...] = src_ref[...].astype(BF16)
    if side_mods:
        w = mw_ref[...]
        brow = lax.broadcasted_iota(jnp.int32, mo_ref.shape, 0)
        acc = jnp.zeros(mo_ref.shape, F32)
        for b in range(batch):
            acc = jnp.where(brow == b, jnp.sum(w * cb_ref[b], axis=0, keepdims=True), acc)
        mo_ref[...] = acc + mb_ref[...]

    wcat_ref[:, 0 * LANE:2 * LANE] = wa_ref[...].astype(BF16)
    wcat_ref[:, 2 * LANE:3 * LANE] = wb0_ref[...].astype(BF16)
    wcat_ref[:, 3 * LANE:4 * LANE] = wb1_ref[...].astype(BF16)
    ab = _dot(h_ref[...], wcat_ref[...])
    a = ab[:, :2 * LANE]
    b = ab[:, 2 * LANE:]
    g = _silu(a) * b
    w2_hi = w2_ref[LANE:2 * LANE, :]
    if has_tail:
        live = j < nj - 1
        col = lax.broadcasted_iota(jnp.int32, g.shape, 1)
        g = jnp.where((col < LANE) | live, g, 0.0)
        w2_hi = jnp.where(live, w2_hi, 0.0)
    gb = g.astype(BF16)
    w2c_ref[0:LANE, :] = w2_ref[0:LANE, :].astype(BF16)
    w2c_ref[LANE:2 * LANE, :] = w2_hi.astype(BF16)
    nc = 512
    for n0 in range(0, d, nc):
        half_gate = 0.5 * mod_ref[k0 + 2:k0 + 3, n0:n0 + nc]
        o_ref[:, n0:n0 + nc] += half_gate * _dot(gb, w2c_ref[:, n0:n0 + nc])

    if final:
        @pl.when(j == nj - 1)
        def _():
            fw = fw_ref[...]

            def fin(r, carry):
                rows = pl.ds(pl.multiple_of(r * NORM_ROWS, NORM_ROWS), NORM_ROWS)
                xn = o_ref[rows, :]
                ms = jnp.mean(xn * xn, axis=-1, keepdims=True)
                o_ref[rows, :] = xn * lax.rsqrt(ms + EPS) * fw
                return carry

            lax.fori_loop(0, tm // NORM_ROWS, fin, 0, unroll=2)


def _ffn_call(x2, mods3, norm_w, final_w, w13, w2, *, layer, half, sub, k0, final, tiles_per_batch, tm,
              cast_src=None, next_mods=None):
    m, d = x2.shape
    d_ff = w2.shape[2]
    nblk = d_ff // LANE
    nj = (nblk + 1) // 2
    has_tail = nblk % 2 == 1
    last = nblk - 1
    l, s = layer, half
    batch = mods3.shape[0]
    kern = functools.partial(_ffn_kernel, sub=sub, k0=k0, nj=nj, has_tail=has_tail, final=final, tm=tm, d=d,
                             side_cast=cast_src is not None, side_mods=next_mods is not None, batch=batch)
    extra_in, extra_out_spec, extra_out_shape, extra_args, extra_scratch = [], [], [], [], []
    assert cast_src is None or next_mods is None
    if cast_src is not None and (cast_src.shape[1] // CAST_ROWS) > (m // tm) * nj:
        return _ffn_call(x2, mods3, norm_w, final_w, w13, w2, layer=layer, half=half, sub=sub, k0=k0, final=final,
                         tiles_per_batch=tiles_per_batch, tm=tm), cast_src[layer].astype(BF16)
    if next_mods is not None:
        c8, mod_w, mod_b = next_mods
        n = mod_w.shape[2]
        nmb = n // LANE
        assert nmb <= (m // tm) * nj and c8.shape[0] == 8
        mblk = lambda i, j: jnp.minimum(i * nj + j, nmb - 1)
        extra_in = [
            pl.BlockSpec((8, d), lambda i, j: (0, 0)),
            pl.BlockSpec((None, d, LANE), lambda i, j: (l + 1, 0, mblk(i, j))),
            pl.BlockSpec((None, 1, LANE), lambda i, j: (l + 1, 0, mblk(i, j))),
        ]
        extra_out_spec = [pl.BlockSpec((8, LANE), lambda i, j: (0, mblk(i, j)))]
        extra_out_shape = [jax.ShapeDtypeStruct((8, n), F32)]
        extra_args = [c8, mod_w, mod_b.reshape(mod_b.shape[0], 1, n)]
        extra_scratch = [pltpu.VMEM((batch, d, LANE), F32)]
    if cast_src is not None:
        rows = cast_src.shape[1]
        n_cast = rows // CAST_ROWS
        assert rows % CAST_ROWS == 0 and cast_src.shape[2] == d
        cast_blk = lambda i, j: jnp.minimum(i * nj + j, n_cast - 1)
        extra_in = [pl.BlockSpec((None, CAST_ROWS, d), lambda i, j: (l, cast_blk(i, j), 0))]
        extra_out_spec = [pl.BlockSpec((CAST_ROWS, d), lambda i, j: (cast_blk(i, j), 0))]
        extra_out_shape = [jax.ShapeDtypeStruct((rows, d), BF16)]
        extra_args = [cast_src]
    out = pl.pallas_call(
        kern,
        grid=(m // tm, nj),
        in_specs=[
            pl.BlockSpec((tm, d), lambda i, j: (i, 0)),
            pl.BlockSpec((None, N_MOD, d), lambda i, j: (i // tiles_per_batch, 0, 0)),
            pl.BlockSpec((None, 3, d), lambda i, j: (l, 0, 0)),
            pl.BlockSpec((1, d), lambda i, j: (0, 0)),
            pl.BlockSpec((None, None, d, 2 * LANE), lambda i, j: (l, s, 0, j)),
            pl.BlockSpec((None, None, d, LANE), lambda i, j: (l, s, 0, nblk + 2 * j)),
            pl.BlockSpec((None, None, d, LANE), lambda i, j: (l, s, 0, nblk + jnp.minimum(2 * j + 1, last))),
            pl.BlockSpec((None, None, 2 * LANE, d), lambda i, j: (l, s, j, 0)),
        ] + extra_in,
        out_specs=[pl.BlockSpec((tm, d), lambda i, j: (i, 0))] + extra_out_spec,
        out_shape=[jax.ShapeDtypeStruct((m, d), F32)] + extra_out_shape,
        scratch_shapes=[
            pltpu.VMEM((tm, d), BF16),
            pltpu.VMEM((d, 4 * LANE), BF16),
            pltpu.VMEM((2 * LANE, d), BF16),
        ] + extra_scratch,
        compiler_params=pltpu.CompilerParams(
            dimension_semantics=("arbitrary", "arbitrary"), vmem_limit_bytes=VMEM_LIMIT),
        name=f"ffn_{sub}",
    )(x2, mods3, norm_w, final_w, w13, w13, w13, w2, *extra_args)
    return out if extra_args else out[0]


def _inproj_kernel(x_ref, mod_ref, nw_ref, w_ref, o_ref, dt_ref, h_ref, *, tm, tn, segments, dt_row):
    shift = mod_ref[3:4, :]
    scale = mod_ref[4:5, :]
    mult = nw_ref[1:2, :] * (1.0 + scale)
    _norm_modulate_rows(x_ref, h_ref, mult, shift, tm)
    h = h_ref[...]
    dt_ref[...] = _dot_nt(h, w_ref[dt_row:dt_row + LANE, :])
    for col0, row0, width in segments:
        for n0 in range(0, width, tn):
            o_ref[:, col0 + n0:col0 + n0 + tn] = _dot_nt(h, w_ref[row0 + n0:row0 + n0 + tn, :]).astype(BF16)


def _inproj_call(x2, mods3, norm_w, w_t, *, layer, tiles_per_batch, tm, d_ssd, xbc_w, n_heads, d_ret):
    m, d = x2.shape
    n_in = w_t.shape[0]
    tr = tm // 2
    tpb = tiles_per_batch * 2
    l = layer
    o_dt = d_ssd + xbc_w
    o_q = o_dt + n_heads
    bc_w = xbc_w - d_ssd
    segments = ((0, 0, 2 * d_ssd), (2 * d_ssd, o_q, 4 * d_ret), (2 * d_ssd + 4 * d_ret, 2 * d_ssd, bc_w))
    n = 2 * d_ssd + 4 * d_ret + bc_w
    assert all(r % 16 == 0 and w % 512 == 0 for _, r, w in segments) and o_dt + LANE <= n_in
    return pl.pallas_call(
        functools.partial(_inproj_kernel, tm=tr, tn=512, segments=segments, dt_row=o_dt),
        grid=(m // tr,),
        in_specs=[
            pl.BlockSpec((tr, d), lambda i: (i, 0)),
            pl.BlockSpec((None, N_MOD, d), lambda i: (i // tpb, 0, 0)),
            pl.BlockSpec((None, 3, d), lambda i: (l, 0, 0)),
            pl.BlockSpec((n_in, d), lambda i: (0, 0), pipeline_mode=pl.Buffered(1)),
        ],
        out_specs=[
            pl.BlockSpec((tr, n), lambda i: (i, 0)),
            pl.BlockSpec((tr, LANE), lambda i: (i, 0)),
        ],
        out_shape=[
            jax.ShapeDtypeStruct((m, n), BF16),
            jax.ShapeDtypeStruct((m, LANE), F32),
        ],
        scratch_shapes=[pltpu.VMEM((tr, d), BF16)],
        compiler_params=pltpu.CompilerParams(
            dimension_semantics=("arbitrary",), vmem_limit_bytes=VMEM_LIMIT),
        name="inproj",
    )(x2, mods3, norm_w, w_t)


def _ssd_body(z_ref, xs_ref, bc_ref, pxs_ref, pbc_ref, dt_ref, cw_ref, cb_ref, dtb_ref, alog_ref, dskip_ref, nw_ref,
              o_ref, state_ref, y_ref, *, n_heads, d_ssd, first_chunk):
    gw = SSD_GROUPS * SSD_STATE
    hpg = n_heads // SSD_GROUPS
    gch = hpg * SSD_HEAD_DIM

    srow = lax.broadcasted_iota(jnp.int32, ((SSD_CONV - 1) * CHUNK, 2 * CHUNK), 0)
    scol = lax.broadcasted_iota(jnp.int32, ((SSD_CONV - 1) * CHUNK, 2 * CHUNK), 1)
    src = (srow & (CHUNK - 1)) - (srow // CHUNK + 1) + CHUNK
    keep = (scol >= CHUNK) | jnp.logical_not(first_chunk)
    shift = jnp.where((scol == src) & keep, 1.0, 0.0).astype(BF16)

    def conv_part(cur_ref, prev_ref, lo, hi):
        cur = cur_ref[...]
        sh = _dot(shift, jnp.concatenate([prev_ref[...], cur], axis=0))
        acc = cb_ref[:, lo:hi] + cw_ref[SSD_CONV - 1:SSD_CONV, lo:hi] * cur.astype(F32)
        for dd in range(1, SSD_CONV):
            k = SSD_CONV - 1 - dd
            acc = acc + cw_ref[k:k + 1, lo:hi] * sh[(dd - 1) * CHUNK:dd * CHUNK, :]
        return acc

    conv_xs = conv_part(xs_ref, pxs_ref, 0, d_ssd)
    conv_bc = conv_part(bc_ref, pbc_ref, d_ssd, d_ssd + 2 * gw)
    xs_a = _silu(conv_xs)
    bc_a = _silu(conv_bc)
    bm = bc_a[:, :gw]
    cm = bc_a[:, gw:]

    lane = lax.broadcasted_iota(jnp.int32, (CHUNK, LANE), 1)
    row = lax.broadcasted_iota(jnp.int32, (CHUNK, LANE), 0)
    head_lane = lane < n_heads
    dt_in = dt_ref[...] + dtb_ref[...]
    dt = jnp.maximum(dt_in, 0.0) + jnp.log1p(jnp.exp(-jnp.abs(dt_in)))
    dt = jnp.where(head_lane, dt, 0.0)
    a = dt * (-jnp.exp(alog_ref[...]))
    causal = row >= lane
    tri = jnp.where(causal, 1.0, 0.0).astype(BF16)
    a_cum = _dot_exact_rhs(tri, a)
    a_cum_t = a_cum.T
    ea = jnp.exp(a_cum)
    dec = jnp.exp(a_cum[CHUNK - 1:CHUNK, :] - a_cum)

    erow = lax.broadcasted_iota(jnp.int32, (LANE, d_ssd), 0)
    ecol = lax.broadcasted_iota(jnp.int32, (LANE, d_ssd), 1)
    expand = jnp.where(ecol // SSD_HEAD_DIM == erow, 1.0, 0.0).astype(BF16)
    dt_x = _dot_hi_mid_lhs(dt, expand)
    dtdec_x = _dot_hi_mid_lhs(dt * dec, expand)
    ea_x = _dot_exact_lhs(ea, expand)
    xdt_b = (xs_a * dt_x).astype(BF16)
    xsd_b = (xs_a * dtdec_x).astype(BF16)
    dskip = dskip_ref[...]

    lo_half = lane < SSD_HEAD_DIM
    pairs = hpg // 2
    for g in range(SSD_GROUPS):
        bg = bm[:, g * SSD_STATE:(g + 1) * SSD_STATE].astype(BF16)
        cg = cm[:, g * SSD_STATE:(g + 1) * SSD_STATE].astype(BF16)
        cb = lax.dot_general(cg, bg, (((1,), (1,)), ((), ())), preferred_element_type=F32)
        gsl = slice(g * gch, (g + 1) * gch)
        prev_t = state_ref[g]
        y_off = _dot(cg, prev_t.astype(BF16)) * ea_x[:, gsl]
        states_t = lax.dot_general(bg, xsd_b[:, gsl], (((0,), (0,)), ((), ())),
                                   preferred_element_type=F32)
        state_ref[g] = prev_t * ea_x[CHUNK - 1:CHUNK, gsl] + states_t
        for p in range(pairs):
            t = g * pairs + p
            tsl = slice(t * LANE, (t + 1) * LANE)
            xt = xdt_b[:, tsl]
            yd = None
            for q in range(2):
                h = 2 * t + q
                seg = a_cum[:, h:h + 1] - a_cum_t[h:h + 1, :]
                lmat = jnp.exp(jnp.where(causal, seg, -jnp.inf))
                mh = (cb * lmat).astype(BF16)
                xh = jnp.where(lo_half if q == 0 else jnp.logical_not(lo_half), xt, jnp.zeros_like(xt))
                part = _dot(mh, xh)
                yd = part if yd is None else yd + part
            y_ref[:, tsl] = yd + y_off[:, p * LANE:(p + 1) * LANE] + xs_a[:, tsl] * dskip[:, tsl]

    yz = y_ref[...] * _silu(z_ref[...].astype(F32))
    ms = jnp.mean(yz * yz, axis=-1, keepdims=True)
    o_ref[:, :d_ssd] = (yz * lax.rsqrt(ms + EPS) * nw_ref[...]).astype(BF16)


def _ret_body(q_ref, k_ref, v_ref, g_ref, cos_ref, sin_ref, gw_ref, gb_ref, o_ref, state_ref, qd_ref, kd_ref,
              *, n_heads, log_g, col0):
    dh = RET_HEAD_DIM
    cosf = cos_ref[...]
    sinf = sin_ref[...]
    row = lax.broadcasted_iota(jnp.int32, (CHUNK, CHUNK), 0)
    col = lax.broadcasted_iota(jnp.int32, (CHUNK, CHUNK), 1)
    causal = row >= col
    for h in range(n_heads):
        hs = slice(h * dh, (h + 1) * dh)
        qh = q_ref[:, hs].astype(F32)
        kh = k_ref[:, hs].astype(F32)
        qt = ((qh * cosf + pltpu.roll(qh, dh // 2, 1) * sinf) * qd_ref[h]).astype(BF16)
        kt = ((kh * cosf + pltpu.roll(kh, dh // 2, 1) * sinf) * kd_ref[h]).astype(BF16)
        vh = v_ref[:, hs]
        scores = lax.dot_general(qt, kt, (((1,), (1,)), ((), ())), preferred_element_type=F32)
        inner = _dot(jnp.where(causal, scores, 0.0).astype(BF16), vh)
        prev = state_ref[h]
        cross = _dot(qt, prev.astype(BF16))
        kv = lax.dot_general(kt, vh, (((0,), (0,)), ((), ())), preferred_element_type=F32)
        state_ref[h] = (prev + kv) * math.exp(CHUNK * log_g[h])
        o = inner + cross
        mu = jnp.mean(o, axis=-1, keepdims=True)
        oc = o - mu
        var = jnp.mean(oc * oc, axis=-1, keepdims=True)
        on = oc * lax.rsqrt(var + EPS) * gw_ref[:, hs] + gb_ref[:, hs]
        o_ref[:, col0 + h * dh:col0 + (h + 1) * dh] = (_silu(g_ref[:, hs].astype(F32)) * on).astype(BF16)


def _mixer_kernel(z_ref, xs_ref, bc_ref, pxs_ref, pbc_ref, dt_ref, q_ref, k_ref, v_ref, g_ref, cos_ref, sin_ref,
                  cw_ref, cb_ref, dtb_ref, alog_ref, dskip_ref, nw_ref, gw_ref, gb_ref,
                  o_ref, sstate_ref, y_ref, rstate_ref, qd_ref, kd_ref,
                  *, n_heads, d_ssd, ret_heads, log_g, batch):
    c = pl.program_id(0)

    @pl.when(c == 0)
    def _():
        pos = lax.broadcasted_iota(jnp.int32, (CHUNK, RET_HEAD_DIM), 0).astype(F32) + 1.0
        for h in range(ret_heads):
            qd_ref[h] = jnp.exp(pos * log_g[h])
            kd_ref[h] = jnp.exp(pos * (-log_g[h])) * (RET_HEAD_DIM ** -0.5)
        sstate_ref[...] = jnp.zeros(sstate_ref.shape, F32)
        rstate_ref[...] = jnp.zeros(rstate_ref.shape, F32)

    for b in range(batch):
        _ssd_body(z_ref.at[b], xs_ref.at[b], bc_ref.at[b], pxs_ref.at[b], pbc_ref.at[b], dt_ref.at[b],
                  cw_ref, cb_ref, dtb_ref, alog_ref, dskip_ref, nw_ref,
                  o_ref.at[b], sstate_ref.at[b], y_ref.at[b], n_heads=n_heads, d_ssd=d_ssd, first_chunk=c == 0)
        _ret_body(q_ref.at[b], k_ref.at[b], v_ref.at[b], g_ref.at[b], cos_ref, sin_ref, gw_ref, gb_ref,
                  o_ref.at[b], rstate_ref.at[b], qd_ref, kd_ref, n_heads=ret_heads, log_g=log_g, col0=d_ssd)


def _mixer_call(proj, dt_raw, cos_t, sin_t, conv_w, conv_b, dt_bias_p, a_log_p, d_skip_x, ssd_norm_w, gn_w, gn_b,
                *, layer, batch, seq, d_ssd, n_heads, d_ret, ret_heads):
    nc = seq // CHUNK
    l = layer
    depth = conv_w.shape[0]
    xbc_w = conv_w.shape[2]
    bc_w = xbc_w - d_ssd
    bc_blk = (2 * d_ssd + 4 * d_ret) // bc_w
    hpg = n_heads // SSD_GROUPS
    assert d_ssd == d_ret
    log_g = tuple(math.log(1.0 - 2.0 ** (-5.0 - h)) for h in range(ret_heads))
    proj3 = proj.reshape(batch, seq, proj.shape[1])
    dt3 = dt_raw.reshape(batch, seq, LANE)
    tok = lambda blk: (lambda c: (0, c, blk))
    before = lambda blk: (lambda c: (0, jnp.maximum(c - 1, 0), blk))
    par = lambda c: (l, 0, 0)
    out = pl.pallas_call(
        functools.partial(_mixer_kernel, n_heads=n_heads, d_ssd=d_ssd, ret_heads=ret_heads, log_g=log_g,
                          batch=batch),
        grid=(nc,),
        in_specs=[
            pl.BlockSpec((batch, CHUNK, d_ssd), tok(0)),
            pl.BlockSpec((batch, CHUNK, d_ssd), tok(1)),
            pl.BlockSpec((batch, CHUNK, bc_w), tok(bc_blk)),
            pl.BlockSpec((batch, CHUNK, d_ssd), before(1)),
            pl.BlockSpec((batch, CHUNK, bc_w), before(bc_blk)),
            pl.BlockSpec((batch, CHUNK, LANE), tok(0)),
            pl.BlockSpec((batch, CHUNK, d_ret), tok(2)),
            pl.BlockSpec((batch, CHUNK, d_ret), tok(3)),
            pl.BlockSpec((batch, CHUNK, d_ret), tok(4)),
            pl.BlockSpec((batch, CHUNK, d_ret), tok(5)),
            pl.BlockSpec((CHUNK, RET_HEAD_DIM), lambda c: (c, 0)),
            pl.BlockSpec((CHUNK, RET_HEAD_DIM), lambda c: (c, 0)),
            pl.BlockSpec((None, SSD_CONV, xbc_w), par),
            pl.BlockSpec((None, 1, xbc_w), par),
            pl.BlockSpec((None, 1, LANE), par),
            pl.BlockSpec((None, 1, LANE), par),
            pl.BlockSpec((None, 1, d_ssd), par),
            pl.BlockSpec((None, 1, d_ssd), par),
            pl.BlockSpec((None, 1, d_ret), par),
            pl.BlockSpec((None, 1, d_ret), par),
        ],
        out_specs=pl.BlockSpec((batch, CHUNK, d_ssd + d_ret), lambda c: (0, c, 0)),
        out_shape=jax.ShapeDtypeStruct((batch, seq, d_ssd + d_ret), BF16),
        scratch_shapes=[
            pltpu.VMEM((batch, SSD_GROUPS, SSD_STATE, hpg * SSD_HEAD_DIM), F32),
            pltpu.VMEM((batch, CHUNK, d_ssd), F32),
            pltpu.VMEM((batch, ret_heads, RET_HEAD_DIM, RET_HEAD_DIM), F32),
            pltpu.VMEM((ret_heads, CHUNK, RET_HEAD_DIM), F32),
            pltpu.VMEM((ret_heads, CHUNK, RET_HEAD_DIM), F32),
        ],
        compiler_params=pltpu.CompilerParams(
            dimension_semantics=("arbitrary",), vmem_limit_bytes=VMEM_LIMIT),
        name="mixer",
    )(proj3, proj3, proj3, proj3, proj3, dt3, proj3, proj3, proj3, proj3, cos_t, sin_t,
      conv_w, conv_b.reshape(depth, 1, xbc_w), dt_bias_p, a_log_p, d_skip_x,
      ssd_norm_w.reshape(depth, 1, d_ssd), gn_w.reshape(depth, 1, d_ret), gn_b.reshape(depth, 1, d_ret))
    return out.reshape(batch * seq, d_ssd + d_ret)


def _outproj_kernel(x_ref, mod_ref, y_ref, w_ref, o_ref, wb_ref, *, tn):
    @pl.when(pl.program_id(0) == 0)
    def _():
        def cast(r, carry):
            rows = pl.ds(pl.multiple_of(r * NORM_ROWS, NORM_ROWS), NORM_ROWS)
            wb_ref[rows, :] = w_ref[rows, :].astype(BF16)
            return carry

        lax.fori_loop(0, w_ref.shape[0] // NORM_ROWS, cast, 0)

    y = y_ref[...]
    for n0 in range(0, o_ref.shape[1], tn):
        cols = slice(n0, n0 + tn)
        o_ref[:, cols] = x_ref[:, cols] + mod_ref[5:6, cols] * _dot(y, wb_ref[:, cols])


def _outproj_call(x2, mods3, y_mix, w_out, *, layer, tiles_per_batch, tm):
    m, d = x2.shape
    d_mix = y_mix.shape[1]
    tr = tm // 2
    tpb = tiles_per_batch * 2
    l = layer
    return pl.pallas_call(
        functools.partial(_outproj_kernel, tn=512),
        grid=(m // tr,),
        in_specs=[
            pl.BlockSpec((tr, d), lambda i: (i, 0)),
            pl.BlockSpec((None, N_MOD, d), lambda i: (i // tpb, 0, 0)),
            pl.BlockSpec((tr, d_mix), lambda i: (i, 0)),
            pl.BlockSpec((None, d_mix, d), lambda i: (l, 0, 0), pipeline_mode=pl.Buffered(1)),
        ],
        out_specs=pl.BlockSpec((tr, d), lambda i: (i, 0)),
        out_shape=jax.ShapeDtypeStruct((m, d), F32),
        scratch_shapes=[pltpu.VMEM((d_mix, d), BF16)],
        compiler_params=pltpu.CompilerParams(
            dimension_semantics=("arbitrary",), vmem_limit_bytes=VMEM_LIMIT),
        name="outproj",
    )(x2, mods3, y_mix, w_out)


def _pad_lanes(v):
    return jnp.pad(v, ((0, 0), (0, LANE - v.shape[1])))[:, None, :]


def kernel(x, c, norm_w, mod_w, mod_b, ffn_w13, ffn_w2, w_in, conv_w, conv_b, dt_bias, a_log, d_skip,
           ssd_norm_w, ret_norm_w, ret_norm_b, w_out, final_norm_w):
    batch, seq, d = x.shape
    depth = norm_w.shape[0]
    n_heads = dt_bias.shape[1]
    d_ssd = ssd_norm_w.shape[1]
    d_ret = ret_norm_w.shape[1]
    ret_heads = d_ret // RET_HEAD_DIM
    xbc_w = conv_w.shape[2]
    m = batch * seq
    tm = min(ROW_TILE, seq)
    tiles_per_batch = seq // tm

    pad_rows = 16
    c_pad = jnp.pad(c, ((0, pad_rows - batch), (0, 0)))
    ffn_steps = (m // tm) * ((ffn_w2.shape[2] // LANE + 1) // 2)
    hosted = (N_MOD * d) // LANE <= ffn_steps
    mods_all = _mods_call(c_pad, mod_w, mod_b, 1 if hosted else depth)[:, :batch].reshape(-1, batch, N_MOD, d)
    mods = mods_all[0]
    next_mods = (c_pad[:8], mod_w, mod_b)

    w_in_t = jnp.swapaxes(w_in, 1, 2)

    half = RET_HEAD_DIM // 2
    inv = ROPE_BASE ** (-jnp.arange(0, RET_HEAD_DIM, 2, dtype=F32) / RET_HEAD_DIM)
    ang = jnp.arange(seq, dtype=F32)[:, None] * inv[None, :]
    cos_t = jnp.concatenate([jnp.cos(ang), jnp.cos(ang)], axis=-1)
    sin_t = jnp.concatenate([-jnp.sin(ang), jnp.sin(ang)], axis=-1)
    assert cos_t.shape == (seq, 2 * half)

    dt_bias_p = _pad_lanes(dt_bias)
    a_log_p = _pad_lanes(a_log)
    d_skip_x = jnp.repeat(d_skip, SSD_HEAD_DIM, axis=1)[:, None, :]
    final_w = final_norm_w.reshape(1, d)

    x2 = x.reshape(m, d)
    common = dict(tiles_per_batch=tiles_per_batch, tm=tm)
    for l in range(depth):
        x2, w_t = _ffn_call(x2, mods, norm_w, final_w, ffn_w13, ffn_w2, layer=l, half=0, sub=0, k0=0,
                            final=False, cast_src=w_in_t, **common)
        proj, dt_raw = _inproj_call(x2, mods, norm_w, w_t, layer=l, d_ssd=d_ssd, xbc_w=xbc_w, n_heads=n_heads,
                                    d_ret=d_ret, **common)
        y_mix = _mixer_call(proj, dt_raw, cos_t, sin_t, conv_w, conv_b, dt_bias_p, a_log_p, d_skip_x, ssd_norm_w,
                            ret_norm_w, ret_norm_b, layer=l, batch=batch, seq=seq, d_ssd=d_ssd,
                            n_heads=n_heads, d_ret=d_ret, ret_heads=ret_heads)
        x2 = _outproj_call(x2, mods, y_mix, w_out, layer=l, **common)
        if l + 1 < depth and not hosted:
            x2 = _ffn_call(x2, mods, norm_w, final_w, ffn_w13, ffn_w2, layer=l, half=1, sub=2, k0=6,
                           final=False, **common)
            mods = mods_all[l + 1]
        elif l + 1 < depth:
            x2, mods_up = _ffn_call(x2, mods, norm_w, final_w, ffn_w13, ffn_w2, layer=l, half=1, sub=2, k0=6,
                                    final=False, next_mods=next_mods, **common)
            mods = mods_up[:batch].reshape(batch, N_MOD, d)
        else:
            x2 = _ffn_call(x2, mods, norm_w, final_w, ffn_w13, ffn_w2, layer=l, half=1, sub=2, k0=6,
                           final=True, **common)
    return x2.reshape(batch, seq, d)
```

```python
import functools
import math

import jax
import jax.numpy as jnp
from jax import lax
from jax.experimental import pallas as pl
from jax.experimental.pallas import tpu as pltpu

F32 = jnp.float32
BF16 = jnp.bfloat16
EPS = 1e-6
LANE = 128
CHUNK = 128
SSD_HEAD_DIM = 64
SSD_GROUPS = 2
SSD_STATE = 128
SSD_CONV = 4
RET_HEAD_DIM = 128
ROPE_BASE = 10000.0
N_MOD = 9
VMEM_LIMIT = 56 * 1024 * 1024
ROW_TILE = 1024
NORM_ROWS = 32
CAST_ROWS = 48


def _silu(x):
    return x * jax.nn.sigmoid(x)


def _split3(x):
    hi = x.astype(BF16)
    r1 = x - hi.astype(F32)
    mid = r1.astype(BF16)
    lo = (r1 - mid.astype(F32)).astype(BF16)
    return hi, mid, lo


def _dot(a, b):
    return jnp.dot(a, b, preferred_element_type=F32)


def _dot_nt(a, b_t):
    return lax.dot_general(a, b_t, (((1,), (1,)), ((), ())), preferred_element_type=F32)


def _dot_exact_rhs(m_bf16, x):
    hi, mid, lo = _split3(x)
    return _dot(m_bf16, hi) + _dot(m_bf16, mid) + _dot(m_bf16, lo)


def _dot_exact_lhs(x, m_bf16):
    hi, mid, lo = _split3(x)
    return _dot(hi, m_bf16) + _dot(mid, m_bf16) + _dot(lo, m_bf16)


def _dot_hi_mid_lhs(x, m_bf16):
    hi = x.astype(BF16)
    mid = (x - hi.astype(F32)).astype(BF16)
    return _dot(hi, m_bf16) + _dot(mid, m_bf16)


def _norm_modulate_rows(x_ref, h_ref, mult, shift, rows_total, copy_ref=None):
    def body(r, carry):
        rows = pl.ds(pl.multiple_of(r * NORM_ROWS, NORM_ROWS), NORM_ROWS)
        xv = x_ref[rows, :]
        ms = jnp.mean(xv * xv, axis=-1, keepdims=True)
        h_ref[rows, :] = (x_ref[rows, :] * lax.rsqrt(ms + EPS) * mult + shift).astype(BF16)
        if copy_ref is not None:
            copy_ref[rows, :] = xv
        return carry

    lax.fori_loop(0, rows_total // NORM_ROWS, body, 0, unroll=4)


def _mods_kernel(c_ref, w_ref, b_ref, o_ref):
    ca = _silu(c_ref[...]).astype(BF16)
    o_ref[...] = _dot(ca, w_ref[...].astype(BF16)) + b_ref[...]


def _mods_call(c_pad, mod_w, mod_b, layers):
    depth, d, n = mod_w.shape
    rows = c_pad.shape[0]
    tn = 2048
    return pl.pallas_call(
        _mods_kernel,
        grid=(layers, n // tn),
        in_specs=[
            pl.BlockSpec((rows, d), lambda l, j: (0, 0)),
            pl.BlockSpec((None, d, tn), lambda l, j: (l, 0, j)),
            pl.BlockSpec((None, 1, tn), lambda l, j: (l, 0, j)),
        ],
        out_specs=pl.BlockSpec((None, rows, tn), lambda l, j: (l, 0, j)),
        out_shape=jax.ShapeDtypeStruct((layers, rows, n), F32),
        compiler_params=pltpu.CompilerParams(
            dimension_semantics=("arbitrary", "arbitrary"), vmem_limit_bytes=VMEM_LIMIT),
        name="mods",
    )(c_pad, mod_w, mod_b.reshape(depth, 1, n))


def _ffn_kernel(x_ref, mod_ref, nw_ref, fw_ref, wa_ref, wb0_ref, wb1_ref, w2_ref, *rest,
                sub, k0, nj, has_tail, final, tm, d, side_cast, side_mods, batch):
    j = pl.program_id(1)
    if side_cast:
        src_ref, o_ref, dst_ref, h_ref, wcat_ref, w2c_ref = rest
    elif side_mods:
        c_ref, mw_ref, mb_ref, o_ref, mo_ref, h_ref, wcat_ref, w2c_ref, cb_ref = rest

        @pl.when((pl.program_id(0) == 0) & (j == 0))
        def _():
            ct = _silu(c_ref[...]).T
            for b in range(batch):
                cb_ref[b] = jnp.broadcast_to(ct[:, b:b + 1], (d, LANE))
    else:
        o_ref, h_ref, wcat_ref, w2c_ref = rest

    @pl.when(j == 0)
    def _():
        shift = mod_ref[k0:k0 + 1, :]
        scale = mod_ref[k0 + 1:k0 + 2, :]
        mult = nw_ref[sub:sub + 1, :] * (1.0 + scale)
        _norm_modulate_rows(x_ref, h_ref, mult, shift, tm, copy_ref=o_ref)

    if side_cast:
        dst_ref[...] = src_ref[...].astype(BF16)
    if side_mods:
        w = mw_ref[...]
        brow = lax.broadcasted_iota(jnp.int32, mo_ref.shape, 0)
        acc = jnp.zeros(mo_ref.shape, F32)
        for b in range(batch):
            acc = jnp.where(brow == b, jnp.sum(w * cb_ref[b], axis=0, keepdims=True), acc)
        mo_ref[...] = acc + mb_ref[...]

    wcat_ref[:, 0 * LANE:2 * LANE] = wa_ref[...].astype(BF16)
    wcat_ref[:, 2 * LANE:3 * LANE] = wb0_ref[...].astype(BF16)
    wcat_ref[:, 3 * LANE:4 * LANE] = wb1_ref[...].astype(BF16)
    ab = _dot(h_ref[...], wcat_ref[...])
    a = ab[:, :2 * LANE]
    b = ab[:, 2 * LANE:]
    g = _silu(a) * b
    w2_hi = w2_ref[LANE:2 * LANE, :]
    if has_tail:
        live = j < nj - 1
        col = lax.broadcasted_iota(jnp.int32, g.shape, 1)
        g = jnp.where((col < LANE) | live, g, 0.0)
        w2_hi = jnp.where(live, w2_hi, 0.0)
    gb = g.astype(BF16)
    w2c_ref[0:LANE, :] = w2_ref[0:LANE, :].astype(BF16)
    w2c_ref[LANE:2 * LANE, :] = w2_hi.astype(BF16)
    nc = 512
    for n0 in range(0, d, nc):
        half_gate = 0.5 * mod_ref[k0 + 2:k0 + 3, n0:n0 + nc]
        o_ref[:, n0:n0 + nc] += half_gate * _dot(gb, w2c_ref[:, n0:n0 + nc])

    if final:
        @pl.when(j == nj - 1)
        def _():
            fw = fw_ref[...]

            def fin(r, carry):
                rows = pl.ds(pl.multiple_of(r * NORM_ROWS, NORM_ROWS), NORM_ROWS)
                xn = o_ref[rows, :]
                ms = jnp.mean(xn * xn, axis=-1, keepdims=True)
                o_ref[rows, :] = xn * lax.rsqrt(ms + EPS) * fw
                return carry

            lax.fori_loop(0, tm // NORM_ROWS, fin, 0, unroll=2)


def _ffn_call(x2, mods3, norm_w, final_w, w13, w2, *, layer, half, sub, k0, final, tiles_per_batch, tm,
              cast_src=None, next_mods=None):
    m, d = x2.shape
    d_ff = w2.shape[2]
    nblk = d_ff // LANE
    nj = (nblk + 1) // 2
    has_tail = nblk % 2 == 1
    last = nblk - 1
    l, s = layer, half
    batch = mods3.shape[0]
    kern = functools.partial(_ffn_kernel, sub=sub, k0=k0, nj=nj, has_tail=has_tail, final=final, tm=tm, d=d,
                             side_cast=cast_src is not None, side_mods=next_mods is not None, batch=batch)
    extra_in, extra_out_spec, extra_out_shape, extra_args, extra_scratch = [], [], [], [], []
    assert cast_src is None or next_mods is None
    if cast_src is not None and (cast_src.shape[1] // CAST_ROWS) > (m // tm) * nj:
        return _ffn_call(x2, mods3, norm_w, final_w, w13, w2, layer=layer, half=half, sub=sub, k0=k0, final=final,
                         tiles_per_batch=tiles_per_batch, tm=tm), cast_src[layer].astype(BF16)
    if next_mods is not None:
        c8, mod_w, mod_b = next_mods
        n = mod_w.shape[2]
        nmb = n // LANE
        assert nmb <= (m // tm) * nj and c8.shape[0] == 8
        mblk = lambda i, j: jnp.minimum(i * nj + j, nmb - 1)
        extra_in = [
            pl.BlockSpec((8, d), lambda i, j: (0, 0)),
            pl.BlockSpec((None, d, LANE), lambda i, j: (l + 1, 0, mblk(i, j))),
            pl.BlockSpec((None, 1, LANE), lambda i, j: (l + 1, 0, mblk(i, j))),
        ]
        extra_out_spec = [pl.BlockSpec((8, LANE), lambda i, j: (0, mblk(i, j)))]
        extra_out_shape = [jax.ShapeDtypeStruct((8, n), F32)]
        extra_args = [c8, mod_w, mod_b.reshape(mod_b.shape[0], 1, n)]
        extra_scratch = [pltpu.VMEM((batch, d, LANE), F32)]
    if cast_src is not None:
        rows = cast_src.shape[1]
        n_cast = rows // CAST_ROWS
        assert rows % CAST_ROWS == 0 and cast_src.shape[2] == d
        cast_blk = lambda i, j: jnp.minimum(i * nj + j, n_cast - 1)
        extra_in = [pl.BlockSpec((None, CAST_ROWS, d), lambda i, j: (l, cast_blk(i, j), 0))]
        extra_out_spec = [pl.BlockSpec((CAST_ROWS, d), lambda i, j: (cast_blk(i, j), 0))]
        extra_out_shape = [jax.ShapeDtypeStruct((rows, d), BF16)]
        extra_args = [cast_src]
    out = pl.pallas_call(
        kern,
        grid=(m // tm, nj),
        in_specs=[
            pl.BlockSpec((tm, d), lambda i, j: (i, 0)),
            pl.BlockSpec((None, N_MOD, d), lambda i, j: (i // tiles_per_batch, 0, 0)),
            pl.BlockSpec((None, 3, d), lambda i, j: (l, 0, 0)),
            pl.BlockSpec((1, d), lambda i, j: (0, 0)),
            pl.BlockSpec((None, None, d, 2 * LANE), lambda i, j: (l, s, 0, j)),
            pl.BlockSpec((None, None, d, LANE), lambda i, j: (l, s, 0, nblk + 2 * j)),
            pl.BlockSpec((None, None, d, LANE), lambda i, j: (l, s, 0, nblk + jnp.minimum(2 * j + 1, last))),
            pl.BlockSpec((None, None, 2 * LANE, d), lambda i, j: (l, s, j, 0)),
        ] + extra_in,
        out_specs=[pl.BlockSpec((tm, d), lambda i, j: (i, 0))] + extra_out_spec,
        out_shape=[jax.ShapeDtypeStruct((m, d), F32)] + extra_out_shape,
        scratch_shapes=[
            pltpu.VMEM((tm, d), BF16),
            pltpu.VMEM((d, 4 * LANE), BF16),
            pltpu.VMEM((2 * LANE, d), BF16),
        ] + extra_scratch,
        compiler_params=pltpu.CompilerParams(
            dimension_semantics=("arbitrary", "arbitrary"), vmem_limit_bytes=VMEM_LIMIT),
        name=f"ffn_{sub}",
    )(x2, mods3, norm_w, final_w, w13, w13, w13, w2, *extra_args)
    return out if extra_args else out[0]


def _inproj_kernel(x_ref, mod_ref, nw_ref, w_ref, o_ref, dt_ref, h_ref, *, tm, tn, segments, dt_row):
    shift = mod_ref[3:4, :]
    scale = mod_ref[4:5, :]
    mult = nw_ref[1:2, :] * (1.0 + scale)
    _norm_modulate_rows(x_ref, h_ref, mult, shift, tm)
    h = h_ref[...]
    dt_ref[...] = _dot_nt(h, w_ref[dt_row:dt_row + LANE, :])
    for col0, row0, width in segments:
        for n0 in range(0, width, tn):
            o_ref[:, col0 + n0:col0 + n0 + tn] = _dot_nt(h, w_ref[row0 + n0:row0 + n0 + tn, :]).astype(BF16)


def _inproj_call(x2, mods3, norm_w, w_t, *, layer, tiles_per_batch, tm, d_ssd, xbc_w, n_heads, d_ret):
    m, d = x2.shape
    n_in = w_t.shape[0]
    tr = tm // 2
    tpb = tiles_per_batch * 2
    l = layer
    o_dt = d_ssd + xbc_w
    o_q = o_dt + n_heads
    bc_w = xbc_w - d_ssd
    segments = ((0, 0, 2 * d_ssd), (2 * d_ssd, o_q, 4 * d_ret), (2 * d_ssd + 4 * d_ret, 2 * d_ssd, bc_w))
    n = 2 * d_ssd + 4 * d_ret + bc_w
    assert all(r % 16 == 0 and w % 512 == 0 for _, r, w in segments) and o_dt + LANE <= n_in
    return pl.pallas_call(
        functools.partial(_inproj_kernel, tm=tr, tn=512, segments=segments, dt_row=o_dt),
        grid=(m // tr,),
        in_specs=[
            pl.BlockSpec((tr, d), lambda i: (i, 0)),
            pl.BlockSpec((None, N_MOD, d), lambda i: (i // tpb, 0, 0)),
            pl.BlockSpec((None, 3, d), lambda i: (l, 0, 0)),
            pl.BlockSpec((n_in, d), lambda i: (0, 0), pipeline_mode=pl.Buffered(1)),
        ],
        out_specs=[
            pl.BlockSpec((tr, n), lambda i: (i, 0)),
            pl.BlockSpec((tr, LANE), lambda i: (i, 0)),
        ],
        out_shape=[
            jax.ShapeDtypeStruct((m, n), BF16),
            jax.ShapeDtypeStruct((m, LANE), F32),
        ],
        scratch_shapes=[pltpu.VMEM((tr, d), BF16)],
        compiler_params=pltpu.CompilerParams(
            dimension_semantics=("arbitrary",), vmem_limit_bytes=VMEM_LIMIT),
        name="inproj",
    )(x2, mods3, norm_w, w_t)


def _ssd_body(z_ref, xs_ref, bc_ref, pxs_ref, pbc_ref, dt_ref, cw_ref, cb_ref, dtb_ref, alog_ref, dskip_ref, nw_ref,
              o_ref, state_ref, y_ref, *, n_heads, d_ssd, first_chunk):
    gw = SSD_GROUPS * SSD_STATE
    hpg = n_heads // SSD_GROUPS
    gch = hpg * SSD_HEAD_DIM

    srow = lax.broadcasted_iota(jnp.int32, ((SSD_CONV - 1) * CHUNK, 2 * CHUNK), 0)
    scol = lax.broadcasted_iota(jnp.int32, ((SSD_CONV - 1) * CHUNK, 2 * CHUNK), 1)
    src = (srow & (CHUNK - 1)) - (srow // CHUNK + 1) + CHUNK
    keep = (scol >= CHUNK) | jnp.logical_not(first_chunk)
    shift = jnp.where((scol == src) & keep, 1.0, 0.0).astype(BF16)

    def conv_part(cur_ref, prev_ref, lo, hi):
        cur = cur_ref[...]
        sh = _dot(shift, jnp.concatenate([prev_ref[...], cur], axis=0))
        acc = cb_ref[:, lo:hi] + cw_ref[SSD_CONV - 1:SSD_CONV, lo:hi] * cur.astype(F32)
        for dd in range(1, SSD_CONV):
            k = SSD_CONV - 1 - dd
            acc = acc + cw_ref[k:k + 1, lo:hi] * sh[(dd - 1) * CHUNK:dd * CHUNK, :]
        return acc

    conv_xs = conv_part(xs_ref, pxs_ref, 0, d_ssd)
    conv_bc = conv_part(bc_ref, pbc_ref, d_ssd, d_ssd + 2 * gw)
    xs_a = _silu(conv_xs)
    bc_a = _silu(conv_bc)
    bm = bc_a[:, :gw]
    cm = bc_a[:, gw:]

    lane = lax.broadcasted_iota(jnp.int32, (CHUNK, LANE), 1)
    row = lax.broadcasted_iota(jnp.int32, (CHUNK, LANE), 0)
    head_lane = lane < n_heads
    dt_in = dt_ref[...] + dtb_ref[...]
    dt = jnp.maximum(dt_in, 0.0) + jnp.log1p(jnp.exp(-jnp.abs(dt_in)))
    dt = jnp.where(head_lane, dt, 0.0)
    a = dt * (-jnp.exp(alog_ref[...]))
    causal = row >= lane
    tri = jnp.where(causal, 1.0, 0.0).astype(BF16)
    a_cum = _dot_exact_rhs(tri, a)
    a_cum_t = a_cum.T
    ea = jnp.exp(a_cum)
    dec = jnp.exp(a_cum[CHUNK - 1:CHUNK, :] - a_cum)

    erow = lax.broadcasted_iota(jnp.int32, (LANE, d_ssd), 0)
    ecol = lax.broadcasted_iota(jnp.int32, (LANE, d_ssd), 1)
    expand = jnp.where(ecol // SSD_HEAD_DIM == erow, 1.0, 0.0).astype(BF16)
    dt_x = _dot_hi_mid_lhs(dt, expand)
    dtdec_x = _dot_hi_mid_lhs(dt * dec, expand)
    ea_x = _dot_exact_lhs(ea, expand)
    xdt_b = (xs_a * dt_x).astype(BF16)
    xsd_b = (xs_a * dtdec_x).astype(BF16)
    dskip = dskip_ref[...]

    lo_half = lane < SSD_HEAD_DIM
    pairs = hpg // 2
    for g in range(SSD_GROUPS):
        bg = bm[:, g * SSD_STATE:(g + 1) * SSD_STATE].astype(BF16)
        cg = cm[:, g * SSD_STATE:(g + 1) * SSD_STATE].astype(BF16)
        cb = lax.dot_general(cg, bg, (((1,), (1,)), ((), ())), preferred_element_type=F32)
        gsl = slice(g * gch, (g + 1) * gch)
        prev_t = state_ref[g]
        y_off = _dot(cg, prev_t.astype(BF16)) * ea_x[:, gsl]
        states_t = lax.dot_general(bg, xsd_b[:, gsl], (((0,), (0,)), ((), ())),
                                   preferred_element_type=F32)
        state_ref[g] = prev_t * ea_x[CHUNK - 1:CHUNK, gsl] + states_t
        for p in range(pairs):
            t = g * pairs + p
            tsl = slice(t * LANE, (t + 1) * LANE)
            xt = xdt_b[:, tsl]
            yd = None
            for q in range(2):
                h = 2 * t + q
                seg = a_cum[:, h:h + 1] - a_cum_t[h:h + 1, :]
                lmat = jnp.exp(jnp.where(causal, seg, -jnp.inf))
                mh = (cb * lmat).astype(BF16)
                xh = jnp.where(lo_half if q == 0 else jnp.logical_not(lo_half), xt, jnp.zeros_like(xt))
                part = _dot(mh, xh)
                yd = part if yd is None else yd + part
            y_ref[:, tsl] = yd + y_off[:, p * LANE:(p + 1) * LANE] + xs_a[:, tsl] * dskip[:, tsl]

    yz = y_ref[...] * _silu(z_ref[...].astype(F32))
    ms = jnp.mean(yz * yz, axis=-1, keepdims=True)
    o_ref[:, :d_ssd] = (yz * lax.rsqrt(ms + EPS) * nw_ref[...]).astype(BF16)


def _ret_body(q_ref, k_ref, v_ref, g_ref, cos_ref, sin_ref, gw_ref, gb_ref, o_ref, state_ref, qd_ref, kd_ref,
              *, n_heads, log_g, col0):
    dh = RET_HEAD_DIM
    cosf = cos_ref[...]
    sinf = sin_ref[...]
    row = lax.broadcasted_iota(jnp.int32, (CHUNK, CHUNK), 0)
    col = lax.broadcasted_iota(jnp.int32, (CHUNK, CHUNK), 1)
    causal = row >= col
    for h in range(n_heads):
        hs = slice(h * dh, (h + 1) * dh)
        qh = q_ref[:, hs].astype(F32)
        kh = k_ref[:, hs].astype(F32)
        qt = ((qh * cosf + pltpu.roll(qh, dh // 2, 1) * sinf) * qd_ref[h]).astype(BF16)
        kt = ((kh * cosf + pltpu.roll(kh, dh // 2, 1) * sinf) * kd_ref[h]).astype(BF16)
        vh = v_ref[:, hs]
        scores = lax.dot_general(qt, kt, (((1,), (1,)), ((), ())), preferred_element_type=F32)
        inner = _dot(jnp.where(causal, scores, 0.0).astype(BF16), vh)
        prev = state_ref[h]
        cross = _dot(qt, prev.astype(BF16))
        kv = lax.dot_general(kt, vh, (((0,), (0,)), ((), ())), preferred_element_type=F32)
        state_ref[h] = (prev + kv) * math.exp(CHUNK * log_g[h])
        o = inner + cross
        mu = jnp.mean(o, axis=-1, keepdims=True)
        oc = o - mu
        var = jnp.mean(oc * oc, axis=-1, keepdims=True)
        on = oc * lax.rsqrt(var + EPS) * gw_ref[:, hs] + gb_ref[:, hs]
        o_ref[:, col0 + h * dh:col0 + (h + 1) * dh] = (_silu(g_ref[:, hs].astype(F32)) * on).astype(BF16)


def _mixer_kernel(z_ref, xs_ref, bc_ref, pxs_ref, pbc_ref, dt_ref, q_ref, k_ref, v_ref, g_ref, cos_ref, sin_ref,
                  cw_ref, cb_ref, dtb_ref, alog_ref, dskip_ref, nw_ref, gw_ref, gb_ref,
                  o_ref, sstate_ref, y_ref, rstate_ref, qd_ref, kd_ref,
                  *, n_heads, d_ssd, ret_heads, log_g, batch):
    c = pl.program_id(0)

    @pl.when(c == 0)
    def _():
        pos = lax.broadcasted_iota(jnp.int32, (CHUNK, RET_HEAD_DIM), 0).astype(F32) + 1.0
        for h in range(ret_heads):
            qd_ref[h] = jnp.exp(pos * log_g[h])
            kd_ref[h] = jnp.exp(pos * (-log_g[h])) * (RET_HEAD_DIM ** -0.5)
        sstate_ref[...] = jnp.zeros(sstate_ref.shape, F32)
        rstate_ref[...] = jnp.zeros(rstate_ref.shape, F32)

    for b in range(batch):
        _ret_body(q_ref.at[b], k_ref.at[b], v_ref.at[b], g_ref.at[b], cos_ref, sin_ref, gw_ref, gb_ref,
                  o_ref.at[b], rstate_ref.at[b], qd_ref, kd_ref, n_heads=ret_heads, log_g=log_g, col0=d_ssd)
    for b in range(batch):
        _ssd_body(z_ref.at[b], xs_ref.at[b], bc_ref.at[b], pxs_ref.at[b], pbc_ref.at[b], dt_ref.at[b],
                  cw_ref, cb_ref, dtb_ref, alog_ref, dskip_ref, nw_ref,
                  o_ref.at[b], sstate_ref.at[b], y_ref.at[b], n_heads=n_heads, d_ssd=d_ssd, first_chunk=c == 0)


def _mixer_call(proj, dt_raw, cos_t, sin_t, conv_w, conv_b, dt_bias_p, a_log_p, d_skip_x, ssd_norm_w, gn_w, gn_b,
                *, layer, batch, seq, d_ssd, n_heads, d_ret, ret_heads):
    nc = seq // CHUNK
    l = layer
    depth = conv_w.shape[0]
    xbc_w = conv_w.shape[2]
    bc_w = xbc_w - d_ssd
    bc_blk = (2 * d_ssd + 4 * d_ret) // bc_w
    hpg = n_heads // SSD_GROUPS
    assert d_ssd == d_ret
    log_g = tuple(math.log(1.0 - 2.0 ** (-5.0 - h)) for h in range(ret_heads))
    proj3 = proj.reshape(batch, seq, proj.shape[1])
    dt3 = dt_raw.reshape(batch, seq, LANE)
    tok = lambda blk: (lambda c: (0, c, blk))
    before = lambda blk: (lambda c: (0, jnp.maximum(c - 1, 0), blk))
    par = lambda c: (l, 0, 0)
    out = pl.pallas_call(
        functools.partial(_mixer_kernel, n_heads=n_heads, d_ssd=d_ssd, ret_heads=ret_heads, log_g=log_g,
                          batch=batch),
        grid=(nc,),
        in_specs=[
            pl.BlockSpec((batch, CHUNK, d_ssd), tok(0)),
            pl.BlockSpec((batch, CHUNK, d_ssd), tok(1)),
            pl.BlockSpec((batch, CHUNK, bc_w), tok(bc_blk)),
            pl.BlockSpec((batch, CHUNK, d_ssd), before(1)),
            pl.BlockSpec((batch, CHUNK, bc_w), before(bc_blk)),
            pl.BlockSpec((batch, CHUNK, LANE), tok(0)),
            pl.BlockSpec((batch, CHUNK, d_ret), tok(2)),
            pl.BlockSpec((batch, CHUNK, d_ret), tok(3)),
            pl.BlockSpec((batch, CHUNK, d_ret), tok(4)),
            pl.BlockSpec((batch, CHUNK, d_ret), tok(5)),
            pl.BlockSpec((CHUNK, RET_HEAD_DIM), lambda c: (c, 0)),
            pl.BlockSpec((CHUNK, RET_HEAD_DIM), lambda c: (c, 0)),
            pl.BlockSpec((None, SSD_CONV, xbc_w), par),
            pl.BlockSpec((None, 1, xbc_w), par),
            pl.BlockSpec((None, 1, LANE), par),
            pl.BlockSpec((None, 1, LANE), par),
            pl.BlockSpec((None, 1, d_ssd), par),
            pl.BlockSpec((None, 1, d_ssd), par),
            pl.BlockSpec((None, 1, d_ret), par),
            pl.BlockSpec((None, 1, d_ret), par),
        ],
        out_specs=pl.BlockSpec((batch, CHUNK, d_ssd + d_ret), lambda c: (0, c, 0)),
        out_shape=jax.ShapeDtypeStruct((batch, seq, d_ssd + d_ret), BF16),
        scratch_shapes=[
            pltpu.VMEM((batch, SSD_GROUPS, SSD_STATE, hpg * SSD_HEAD_DIM), F32),
            pltpu.VMEM((batch, CHUNK, d_ssd), F32),
            pltpu.VMEM((batch, ret_heads, RET_HEAD_DIM, RET_HEAD_DIM), F32),
            pltpu.VMEM((ret_heads, CHUNK, RET_HEAD_DIM), F32),
            pltpu.VMEM((ret_heads, CHUNK, RET_HEAD_DIM), F32),
        ],
        compiler_params=pltpu.CompilerParams(
            dimension_semantics=("arbitrary",), vmem_limit_bytes=VMEM_LIMIT),
        name="mixer",
    )(proj3, proj3, proj3, proj3, proj3, dt3, proj3, proj3, proj3, proj3, cos_t, sin_t,
      conv_w, conv_b.reshape(depth, 1, xbc_w), dt_bias_p, a_log_p, d_skip_x,
      ssd_norm_w.reshape(depth, 1, d_ssd), gn_w.reshape(depth, 1, d_ret), gn_b.reshape(depth, 1, d_ret))
    return out.reshape(batch * seq, d_ssd + d_ret)


def _outproj_kernel(x_ref, mod_ref, y_ref, w_ref, o_ref, wb_ref, *, tn):
    @pl.when(pl.program_id(0) == 0)
    def _():
        def cast(r, carry):
            rows = pl.ds(pl.multiple_of(r * NORM_ROWS, NORM_ROWS), NORM_ROWS)
            wb_ref[rows, :] = w_ref[rows, :].astype(BF16)
            return carry

        lax.fori_loop(0, w_ref.shape[0] // NORM_ROWS, cast, 0)

    y = y_ref[...]
    for n0 in range(0, o_ref.shape[1], tn):
        cols = slice(n0, n0 + tn)
        o_ref[:, cols] = x_ref[:, cols] + mod_ref[5:6, cols] * _dot(y, wb_ref[:, cols])


def _outproj_call(x2, mods3, y_mix, w_out, *, layer, tiles_per_batch, tm):
    m, d = x2.shape
    d_mix = y_mix.shape[1]
    tr = tm // 2
    tpb = tiles_per_batch * 2
    l = layer
    return pl.pallas_call(
        functools.partial(_outproj_kernel, tn=512),
        grid=(m // tr,),
        in_specs=[
            pl.BlockSpec((tr, d), lambda i: (i, 0)),
            pl.BlockSpec((None, N_MOD, d), lambda i: (i // tpb, 0, 0)),
            pl.BlockSpec((tr, d_mix), lambda i: (i, 0)),
            pl.BlockSpec((None, d_mix, d), lambda i: (l, 0, 0), pipeline_mode=pl.Buffered(1)),
        ],
        out_specs=pl.BlockSpec((tr, d), lambda i: (i, 0)),
        out_shape=jax.ShapeDtypeStruct((m, d), F32),
        scratch_shapes=[pltpu.VMEM((d_mix, d), BF16)],
        compiler_params=pltpu.CompilerParams(
            dimension_semantics=("arbitrary",), vmem_limit_bytes=VMEM_LIMIT),
        name="outproj",
    )(x2, mods3, y_mix, w_out)


def _pad_lanes(v):
    return jnp.pad(v, ((0, 0), (0, LANE - v.shape[1])))[:, None, :]


def kernel(x, c, norm_w, mod_w, mod_b, ffn_w13, ffn_w2, w_in, conv_w, conv_b, dt_bias, a_log, d_skip,
           ssd_norm_w, ret_norm_w, ret_norm_b, w_out, final_norm_w):
    batch, seq, d = x.shape
    depth = norm_w.shape[0]
    n_heads = dt_bias.shape[1]
    d_ssd = ssd_norm_w.shape[1]
    d_ret = ret_norm_w.shape[1]
    ret_heads = d_ret // RET_HEAD_DIM
    xbc_w = conv_w.shape[2]
    m = batch * seq
    tm = min(ROW_TILE, seq)
    tiles_per_batch = seq // tm

    pad_rows = 16
    c_pad = jnp.pad(c, ((0, pad_rows - batch), (0, 0)))
    ffn_steps = (m // tm) * ((ffn_w2.shape[2] // LANE + 1) // 2)
    hosted = (N_MOD * d) // LANE <= ffn_steps
    mods_all = _mods_call(c_pad, mod_w, mod_b, 1 if hosted else depth)[:, :batch].reshape(-1, batch, N_MOD, d)
    mods = mods_all[0]
    next_mods = (c_pad[:8], mod_w, mod_b)

    w_in_t = jnp.swapaxes(w_in, 1, 2)

    half = RET_HEAD_DIM // 2
    inv = ROPE_BASE ** (-jnp.arange(0, RET_HEAD_DIM, 2, dtype=F32) / RET_HEAD_DIM)
    ang = jnp.arange(seq, dtype=F32)[:, None] * inv[None, :]
    cos_t = jnp.concatenate([jnp.cos(ang), jnp.cos(ang)], axis=-1)
    sin_t = jnp.concatenate([-jnp.sin(ang), jnp.sin(ang)], axis=-1)
    assert cos_t.shape == (seq, 2 * half)

    dt_bias_p = _pad_lanes(dt_bias)
    a_log_p = _pad_lanes(a_log)
    d_skip_x = jnp.repeat(d_skip, SSD_HEAD_DIM, axis=1)[:, None, :]
    final_w = final_norm_w.reshape(1, d)

    x2 = x.reshape(m, d)
    common = dict(tiles_per_batch=tiles_per_batch, tm=tm)
    for l in range(depth):
        x2, w_t = _ffn_call(x2, mods, norm_w, final_w, ffn_w13, ffn_w2, layer=l, half=0, sub=0, k0=0,
                            final=False, cast_src=w_in_t, **common)
        proj, dt_raw = _inproj_call(x2, mods, norm_w, w_t, layer=l, d_ssd=d_ssd, xbc_w=xbc_w, n_heads=n_heads,
                                    d_ret=d_ret, **common)
        y_mix = _mixer_call(proj, dt_raw, cos_t, sin_t, conv_w, conv_b, dt_bias_p, a_log_p, d_skip_x, ssd_norm_w,
                            ret_norm_w, ret_norm_b, layer=l, batch=batch, seq=seq, d_ssd=d_ssd,
                            n_heads=n_heads, d_ret=d_ret, ret_heads=ret_heads)
        x2 = _outproj_call(x2, mods, y_mix, w_out, layer=l, **common)
        if l + 1 < depth and not hosted:
            x2 = _ffn_call(x2, mods, norm_w, final_w, ffn_w13, ffn_w2, layer=l, half=1, sub=2, k0=6,
                           final=False, **common)
            mods = mods_all[l + 1]
        elif l + 1 < depth:
            x2, mods_up = _ffn_call(x2, mods, norm_w, final_w, ffn_w13, ffn_w2, layer=l, half=1, sub=2, k0=6,
                                    final=False, next_mods=next_mods, **common)
            mods = mods_up[:batch].reshape(batch, N_MOD, d)
        else:
            x2 = _ffn_call(x2, mods, norm_w, final_w, ffn_w13, ffn_w2, layer=l, half=1, sub=2, k0=6,
                           final=True, **common)
    return x2.reshape(batch, seq, d)
```
